```python
import math
import jax, jax.numpy as jnp
from jax import lax
import numpy as np

D_MODEL = 2048
BATCH = 16
SEQ = 2048
DEPTH = 1

HEAD_DIM = 128
D_MIX = D_MODEL
GLA_HEADS = D_MIX // 2 // HEAD_DIM
GLA_DK = HEAD_DIM // 2
GLA_DV = HEAD_DIM
GLA_RANK = 16
GLA_TAU = 16.0
GLA_CHUNK = 64
FOX_HEADS = D_MIX // 4 // HEAD_DIM
FOX_BLOCK = 128
MEM_HEADS = 4
MEM_TOKENS = 256
N_OUT_HEADS = GLA_HEADS + FOX_HEADS + MEM_HEADS
D_FF = 4 * D_MODEL
EPS = 1e-6

kernel_name = "hymba_gla_fox_memory_layer"


def _in_widths():
    return (
        GLA_HEADS * GLA_DK,
        GLA_HEADS * GLA_DK,
        GLA_HEADS * GLA_DV,
        GLA_HEADS * GLA_DV,
        GLA_RANK,
        FOX_HEADS * HEAD_DIM,
        FOX_HEADS * HEAD_DIM,
        FOX_HEADS * HEAD_DIM,
        FOX_HEADS * HEAD_DIM,
        FOX_HEADS,
        MEM_HEADS * HEAD_DIM,
        MEM_HEADS * HEAD_DIM,
    )


def _split_points():
    pts, acc = [], 0
    for w in _in_widths()[:-1]:
        acc += w
        pts.append(acc)
    return pts


def rms_norm(x, g):
    xf = x.astype(jnp.float32)
    y = xf * lax.rsqrt(jnp.mean(xf * xf, axis=-1, keepdims=True) + EPS)
    return (y * g.astype(jnp.float32)).astype(x.dtype)


def _heads(t, h):
    return t.reshape(t.shape[0], t.shape[1], h, -1).transpose(0, 2, 1, 3)


def gla_chunked(q, k, v, log_a):
    B, H, S, DK = q.shape
    DV = v.shape[-1]
    C = GLA_CHUNK
    n = S // C

    def to_chunks(t):
        return t.reshape(B, H, n, C, t.shape[-1]).transpose(2, 0, 1, 3, 4)

    qc = to_chunks(q * (DK ** -0.5))
    kc, vc, gc = to_chunks(k), to_chunks(v), to_chunks(log_a)
    causal = jnp.tril(jnp.ones((C, C), dtype=bool))[:, :, None]

    def step(state, xs):
        qi, ki, vi, gi = xs
        b = jnp.cumsum(gi.astype(jnp.float32), axis=2)
        inter = jnp.einsum('bhck,bhkv->bhcv', qi * jnp.exp(b), state)
        rel = b[:, :, :, None, :] - b[:, :, None, :, :]
        decay = jnp.exp(jnp.where(causal, rel, -jnp.inf))
        scores = jnp.einsum('bhik,bhijk,bhjk->bhij', qi.astype(jnp.float32), decay,
                            ki.astype(jnp.float32))
        intra = jnp.einsum('bhij,bhjv->bhiv', scores, vi.astype(jnp.float32))
        b_last = b[:, :, -1:, :]
        new_state = (jnp.exp(b_last[:, :, 0, :])[..., None] * state
                     + jnp.einsum('bhck,bhcv->bhkv', ki * jnp.exp(b_last - b),
                                  vi.astype(jnp.float32)))
        return new_state, inter + intra

    state0 = jnp.zeros((B, H, DK, DV), jnp.float32)
    _, out = lax.scan(step, state0, (qc, kc, vc, gc))
    return out.transpose(1, 2, 0, 3, 4).reshape(B, H, S, DV)


def forgetting_attention(q, k, v, log_f):
    B, H, S, D = q.shape
    scale = D ** -0.5
    c = jnp.cumsum(log_f.astype(jnp.float32), axis=-1)
    outs = []
    for i in range(S // FOX_BLOCK):
        q0, q1 = i * FOX_BLOCK, (i + 1) * FOX_BLOCK
        qb, kb, vb = q[:, :, q0:q1], k[:, :, :q1], v[:, :, :q1]
        logits = (jnp.einsum('bhqd,bhkd->bhqk', qb, kb).astype(jnp.float32) * scale
                  + c[:, :, q0:q1, None] - c[:, :, None, :q1])
        mask = (q0 + jnp.arange(FOX_BLOCK))[:, None] >= jnp.arange(q1)[None, :]
        p = jax.nn.softmax(jnp.where(mask, logits, -jnp.inf), axis=-1)
        outs.append(jnp.einsum('bhqk,bhkd->bhqd', p.astype(v.dtype), vb))
    return jnp.concatenate(outs, axis=2)


def setup_inputs(seed: int = 0) -> dict:
    key = jax.random.key(seed)
    ks = jax.random.split(key, 20)
    n = jax.random.normal
    d_in = sum(_in_widths())
    return {
        "x": n(ks[0], (BATCH, SEQ, D_MODEL), jnp.float32),
        "mem": n(ks[1], (BATCH, MEM_TOKENS, D_MODEL), jnp.float32),
        "attn_norm_g": 1.0 + 0.02 * n(ks[2], (D_MODEL,), jnp.float32),
        "w_in": n(ks[3], (D_MODEL, d_in), jnp.float32) * D_MODEL ** -0.5,
        "gla_a_w2": n(ks[4], (GLA_RANK, GLA_HEADS * GLA_DK), jnp.float32) * GLA_RANK ** -0.5,
        "gla_a_b": 0.5 * n(ks[5], (GLA_HEADS * GLA_DK,), jnp.float32),
        "fox_f_b": 3.0 + 0.5 * n(ks[6], (FOX_HEADS,), jnp.float32),
        "fox_q_norm_g": 1.0 + 0.02 * n(ks[7], (HEAD_DIM,), jnp.float32),
        "fox_k_norm_g": 1.0 + 0.02 * n(ks[8], (HEAD_DIM,), jnp.float32),
        "mem_norm_g": 1.0 + 0.02 * n(ks[9], (D_MODEL,), jnp.float32),
        "w_mem_kv": n(ks[10], (D_MODEL, 2 * MEM_HEADS * HEAD_DIM), jnp.float32) * D_MODEL ** -0.5,
        "mem_q_norm_g": 1.0 + 0.02 * n(ks[11], (HEAD_DIM,), jnp.float32),
        "mem_k_norm_g": 1.0 + 0.02 * n(ks[12], (HEAD_DIM,), jnp.float32),
        "out_norm_g": 1.0 + 0.02 * n(ks[13], (N_OUT_HEADS * HEAD_DIM,), jnp.float32),
        "w_out": n(ks[14], (N_OUT_HEADS * HEAD_DIM, D_MODEL), jnp.float32) * (N_OUT_HEADS * HEAD_DIM) ** -0.5,
        "mlp_norm_g": 1.0 + 0.02 * n(ks[15], (D_MODEL,), jnp.float32),
        "w_up": n(ks[16], (D_MODEL, D_FF), jnp.float32) * D_MODEL ** -0.5,
        "w_down": n(ks[17], (D_FF, D_MODEL), jnp.float32) * D_FF ** -0.5,
    }


def reference(x, mem, attn_norm_g, w_in, gla_a_w2, gla_a_b, fox_f_b, fox_q_norm_g,
              fox_k_norm_g, mem_norm_g, w_mem_kv, mem_q_norm_g, mem_k_norm_g,
              out_norm_g, w_out, mlp_norm_g, w_up, w_down):
    B, S, _ = x.shape
    h = x
    for _layer in range(DEPTH):
        xn = rms_norm(h, attn_norm_g)
        proj = xn @ w_in
        (gq, gk, gv, gg, ga, fq, fk, fv, fg, ff, mq, mg) = jnp.split(proj, _split_points(), axis=-1)

        log_a = jax.nn.log_sigmoid((ga @ gla_a_w2 + gla_a_b).astype(jnp.float32)) / GLA_TAU
        gla_o = gla_chunked(_heads(gq, GLA_HEADS), _heads(gk, GLA_HEADS),
                            _heads(gv, GLA_HEADS), _heads(log_a, GLA_HEADS))
        gla_o = gla_o.astype(x.dtype).transpose(0, 2, 1, 3)

        fq_h = rms_norm(_heads(fq, FOX_HEADS), fox_q_norm_g)
        fk_h = rms_norm(_heads(fk, FOX_HEADS), fox_k_norm_g)
        log_f = jax.nn.log_sigmoid((ff + fox_f_b).astype(jnp.float32)).transpose(0, 2, 1)
        fox_o = forgetting_attention(fq_h, fk_h, _heads(fv, FOX_HEADS), log_f)
        fox_o = fox_o.astype(x.dtype).transpose(0, 2, 1, 3)

        mn = rms_norm(mem, mem_norm_g)
        mk, mv = jnp.split(mn @ w_mem_kv, 2, axis=-1)
        mk = rms_norm(mk.reshape(B, mem.shape[1], MEM_HEADS, HEAD_DIM), mem_k_norm_g)
        mv = mv.reshape(B, mem.shape[1], MEM_HEADS, HEAD_DIM)
        mq_h = rms_norm(mq.reshape(B, S, MEM_HEADS, HEAD_DIM), mem_q_norm_g)
        m_logits = jnp.einsum('bshd,bmhd->bhsm', mq_h, mk).astype(jnp.float32) * HEAD_DIM ** -0.5
        m_p = jax.nn.softmax(m_logits, axis=-1)
        mem_o = jnp.einsum('bhsm,bmhd->bshd', m_p.astype(mv.dtype), mv)

        o = jnp.concatenate([gla_o, fox_o, mem_o], axis=2)
        o = rms_norm(o, out_norm_g.reshape(N_OUT_HEADS, HEAD_DIM)).reshape(B, S, -1)
        gate = jnp.concatenate([jax.nn.silu(gg), jax.nn.sigmoid(fg), jax.nn.sigmoid(mg)], axis=-1)
        h = h + (o * gate) @ w_out

        u = jax.nn.relu(rms_norm(h, mlp_norm_g) @ w_up)
        h = h + (u * u) @ w_down
    return h
```

```python
import functools

import jax
import jax.numpy as jnp
from jax import lax
from jax.experimental import pallas as pl
from jax.experimental.pallas import tpu as pltpu

F32 = jnp.float32
BF16 = jnp.bfloat16

HEAD_DIM = 128
GLA_HEADS = 8
GLA_DK = 64
GLA_RANK = 16
GLA_TAU = 16.0
FOX_HEADS = 4
MEM_HEADS = 4
EPS = 1e-6

LANES = 128
GLA_CHUNK = 64
GLA_STEP = 128
GLA_SAFE_EXP = 60.0
FOX_TQ = 256
MEM_TQ = 512
NEG_BIG = -1e30
VMEM_LIMIT = 50 * 1024 * 1024

OFF_GQ, OFF_GK, OFF_GV, OFF_GG = 0, 512, 1024, 2048
OFF_FQ, OFF_FK, OFF_FV, OFF_FG = 3072, 3584, 4096, 4608
OFF_MQ, OFF_MG = 5120, 5632
N_MAIN = 6144
SMALL_FF = GLA_RANK


def _dot(a, b):
    return jnp.dot(a, b, preferred_element_type=F32)


def _dot_nt(a, b):
    return lax.dot_general(a, b, (((1,), (1,)), ((), ())), preferred_element_type=F32)


def _dot_tn(a, b):
    return lax.dot_general(a, b, (((0,), (0,)), ((), ())), preferred_element_type=F32)


def _rms(x, g):
    return x * lax.rsqrt(jnp.mean(x * x, axis=-1, keepdims=True) + EPS) * g


def _log_sigmoid(z):
    return jnp.minimum(z, 0.0) - jnp.log1p(jnp.exp(-jnp.abs(z)))


def _sigmoid(z):
    return 1.0 / (1.0 + jnp.exp(-z))


def _split3(a):
    hi = a.astype(BF16)
    r = a - hi.astype(F32)
    mid = r.astype(BF16)
    lo = (r - mid.astype(F32)).astype(BF16)
    return hi, mid, lo


def _params(sem):
    return pltpu.CompilerParams(dimension_semantics=sem, vmem_limit_bytes=VMEM_LIMIT)


def _inproj_kernel(x_ref, g_ref, wm_ref, ws_ref, main_ref, small_ref, xn_ref):
    @pl.when(pl.program_id(1) == 0)
    def _():
        xn = _rms(x_ref[...], g_ref[...]).astype(BF16)
        xn_ref[...] = xn
        small_ref[...] = _dot(xn, ws_ref[...])

    main_ref[...] = _dot(xn_ref[...], wm_ref[...]).astype(BF16)


def _in_proj(x2, g, w_main, w_small, tm=1024, tn=1024):
    t, d = x2.shape
    return pl.pallas_call(
        _inproj_kernel,
        grid=(t // tm, N_MAIN // tn),
        in_specs=[
            pl.BlockSpec((tm, d), lambda i, j: (i, 0)),
            pl.BlockSpec((1, d), lambda i, j: (0, 0)),
            pl.BlockSpec((d, tn), lambda i, j: (0, j)),
            pl.BlockSpec((d, LANES), lambda i, j: (0, 0)),
        ],
        out_specs=[
            pl.BlockSpec((tm, tn), lambda i, j: (i, j)),
            pl.BlockSpec((tm, LANES), lambda i, j: (i, 0)),
        ],
        out_shape=[
            jax.ShapeDtypeStruct((t, N_MAIN), BF16),
            jax.ShapeDtypeStruct((t, LANES), F32),
        ],
        scratch_shapes=[pltpu.VMEM((tm, d), BF16)],
        compiler_params=_params(("parallel", "arbitrary")),
        name="in_proj",
    )(x2, g, w_main, w_small)


def _gla_kernel(q_ref, k_ref, v_ref, gg_ref, small_ref, w2_ref, ab_ref, ong_ref,
                o_ref, st_ref, qs_scr, bs_scr, os_scr):
    C = GLA_CHUNK
    n_pairs = GLA_HEADS // 2

    @pl.when(pl.program_id(1) == 0)
    def _():
        st_ref[...] = jnp.zeros_like(st_ref)

    z = _dot(small_ref[...].astype(BF16), w2_ref[...]) + ab_ref[...]
    la = _log_sigmoid(z) * (1.0 / GLA_TAU)

    ri = lax.broadcasted_iota(jnp.int32, (GLA_STEP, GLA_STEP), 0)
    ci = lax.broadcasted_iota(jnp.int32, (GLA_STEP, GLA_STEP), 1)
    same_chunk = (ri ^ ci) < C
    tri = jnp.where(same_chunk, jnp.where(ci <= ri, 1.0, 0.0), 0.0).astype(BF16)
    hi, mid, lo = _split3(la)
    bc = _dot(tri, hi) + _dot(tri, mid) + _dot(tri, lo)

    lane = lax.broadcasted_iota(jnp.int32, (C, LANES), 1)
    low_half = lane < GLA_DK
    st_r = lax.broadcasted_iota(jnp.int32, (2 * HEAD_DIM, LANES), 0)
    st_c = lax.broadcasted_iota(jnp.int32, (2 * HEAD_DIM, LANES), 1)
    st_mask = (st_r < HEAD_DIM) == (st_c < GLA_DK)
    a_r = lax.broadcasted_iota(jnp.int32, (C, 2 * C), 0)
    a_c = lax.broadcasted_iota(jnp.int32, (C, 2 * C), 1)
    causal = a_r >= (a_c & (C - 1))
    zeros_v = jnp.zeros((C, HEAD_DIM), BF16)
    jrow = lax.broadcasted_iota(jnp.int32, (C, LANES), 0)

    def unit(ch, p, fast):
        rows = slice(ch * C, (ch + 1) * C)
        kl = slice(p * LANES, (p + 1) * LANES)
        vl = slice(p * 2 * HEAD_DIM, (p + 1) * 2 * HEAD_DIM)
        b = bc[rows, kl]
        qs = q_ref[rows, kl].astype(F32) * (GLA_DK ** -0.5)
        kf = k_ref[rows, kl].astype(F32)
        v2 = v_ref[rows, vl]
        qe16 = (qs * jnp.exp(b)).astype(BF16)
        b_last = b[C - 1:C, :]
        st = st_ref[p]
        o = _dot_nt(qe16, st.astype(BF16))
        if fast:
            ke = kf * jnp.exp(-b)
            kbd = jnp.concatenate(
                [jnp.where(low_half, ke, 0.0), jnp.where(low_half, 0.0, ke)], axis=0).astype(BF16)
            a = jnp.where(causal, _dot_nt(qe16, kbd), 0.0).astype(BF16)
            vbd = jnp.concatenate(
                [jnp.concatenate([v2[:, :HEAD_DIM], zeros_v], axis=1),
                 jnp.concatenate([zeros_v, v2[:, HEAD_DIM:]], axis=1)], axis=0)
            o = o + _dot(a, vbd)
            upd = _dot_tn(v2, ke.astype(BF16))
            st_ref[p] = jnp.exp(b_last) * (st + jnp.where(st_mask, upd, 0.0))
        else:
            kd16 = (kf * jnp.exp(b_last - b)).astype(BF16)
            upd = _dot_tn(v2, kd16)
            st_ref[p] = jnp.exp(b_last) * st + jnp.where(st_mask, upd, 0.0)
            qs_scr[...] = qs
            bs_scr[...] = b
            v2f = v2.astype(F32)

            def row(i, carry):
                qi = qs_scr[pl.ds(i, 1), :]
                bi = bs_scr[pl.ds(i, 1), :]
                w = qi * kf * jnp.exp(jnp.minimum(bi - b, 0.0))
                w = jnp.where(jrow <= i, w, 0.0)
                p0 = jnp.sum(jnp.where(low_half, w, 0.0), axis=1, keepdims=True)
                p1 = jnp.sum(jnp.where(low_half, 0.0, w), axis=1, keepdims=True)
                pv = jnp.concatenate([p0 * v2f[:, :HEAD_DIM], p1 * v2f[:, HEAD_DIM:]], axis=1)
                os_scr[pl.ds(i, 1), :] = jnp.sum(pv, axis=0, keepdims=True)
                return carry

            lax.fori_loop(0, C, row, 0)
            o = o + os_scr[...]
        for hh in range(2):
            hl = slice((2 * p + hh) * HEAD_DIM, (2 * p + hh + 1) * HEAD_DIM)
            on = _rms(o[:, hh * HEAD_DIM:(hh + 1) * HEAD_DIM], ong_ref[:, hl])
            g = gg_ref[rows, hl].astype(F32)
            o_ref[rows, hl] = (on * (g * _sigmoid(g))).astype(BF16)

    def run(fast):
        for ch in range(GLA_STEP // C):
            for p in range(n_pairs):
                unit(ch, p, fast)

    safe = jnp.min(la) * C >= -GLA_SAFE_EXP

    @pl.when(safe)
    def _():
        run(True)

    @pl.when(jnp.logical_not(safe))
    def _():
        run(False)


def _gla(main, small, w2p, ab, ong, batch, seq):
    t = main.shape[0]
    nc = seq // GLA_STEP
    rb = lambda b, c: b * nc + c
    dq = GLA_HEADS * GLA_DK
    dv = GLA_HEADS * HEAD_DIM
    return pl.pallas_call(
        _gla_kernel,
        grid=(batch, nc),
        in_specs=[
            pl.BlockSpec((GLA_STEP, dq), lambda b, c: (rb(b, c), OFF_GQ // dq)),
            pl.BlockSpec((GLA_STEP, dq), lambda b, c: (rb(b, c), OFF_GK // dq)),
            pl.BlockSpec((GLA_STEP, dv), lambda b, c: (rb(b, c), OFF_GV // dv)),
            pl.BlockSpec((GLA_STEP, dv), lambda b, c: (rb(b, c), OFF_GG // dv)),
            pl.BlockSpec((GLA_STEP, LANES), lambda b, c: (rb(b, c), 0)),
            pl.BlockSpec((LANES, dq), lambda b, c: (0, 0)),
            pl.BlockSpec((1, dq), lambda b, c: (0, 0)),
            pl.BlockSpec((1, dv), lambda b, c: (0, 0)),
        ],
        out_specs=pl.BlockSpec((GLA_STEP, dv), lambda b, c: (rb(b, c), 0)),
        out_shape=jax.ShapeDtypeStruct((t, dv), BF16),
        scratch_shapes=[
            pltpu.VMEM((GLA_HEADS // 2, 2 * HEAD_DIM, LANES), F32),
            pltpu.VMEM((GLA_CHUNK, LANES), F32),
            pltpu.VMEM((GLA_CHUNK, LANES), F32),
            pltpu.VMEM((GLA_CHUNK, 2 * HEAD_DIM), F32),
        ],
        compiler_params=_params(("parallel", "arbitrary")),
        name="gla",
    )(main, main, main, main, small, w2p, ab, ong)


def _foxgate_kernel(small_ref, fb_ref, ccol_ref, crow_ref):
    s = small_ref.shape[0]
    blk = 256
    ri = lax.broadcasted_iota(jnp.int32, (blk, blk), 0)
    ci = lax.broadcasted_iota(jnp.int32, (blk, blk), 1)
    tri = jnp.where(ci <= ri, 1.0, 0.0).astype(BF16)
    carry = jnp.zeros((1, LANES), F32)
    for i in range(s // blk):
        rows = slice(i * blk, (i + 1) * blk)
        lf = _log_sigmoid(small_ref[rows, :] + fb_ref[...])
        hi, mid, lo = _split3(lf)
        c = _dot(tri, hi) + _dot(tri, mid) + _dot(tri, lo) + carry
        ccol_ref[rows, :] = c
        carry = c[blk - 1:blk, :]
    crow_ref[...] = ccol_ref[...].T


def _fox_gate(small, fb, batch, seq):
    t = small.shape[0]
    return pl.pallas_call(
        _foxgate_kernel,
        grid=(batch,),
        in_specs=[
            pl.BlockSpec((seq, LANES), lambda b: (b, 0)),
            pl.BlockSpec((1, LANES), lambda b: (0, 0)),
        ],
        out_specs=[
            pl.BlockSpec((seq, LANES), lambda b: (b, 0)),
            pl.BlockSpec((None, LANES, seq), lambda b: (b, 0, 0)),
        ],
        out_shape=[
            jax.ShapeDtypeStruct((t, LANES), F32),
            jax.ShapeDtypeStruct((batch, LANES, seq), F32),
        ],
        compiler_params=_params(("parallel",)),
        name="fox_gate",
    )(small, fb)


def _fox_kernel(q_ref, k_ref, v_ref, fg_ref, ccol_ref, crow_ref, gq_ref, gk_ref, ong_ref,
                o_ref, qn_scr, kn_scr):
    s = q_ref.shape[0]
    h = pl.program_id(1)
    scale = HEAD_DIM ** -0.5
    qn_scr[...] = (_rms(q_ref[...].astype(F32), gq_ref[...]) * scale).astype(BF16)
    kn_scr[...] = _rms(k_ref[...].astype(F32), gk_ref[...]).astype(BF16)
    lane = lax.broadcasted_iota(jnp.int32, (FOX_TQ, LANES), 1)
    pick = lane == SMALL_FF + h
    for qb in range(s // FOX_TQ):
        r0 = qb * FOX_TQ
        n_keys = r0 + FOX_TQ
        rows = slice(r0, n_keys)
        logits = _dot_nt(qn_scr[rows, :], kn_scr[:n_keys, :])
        cq = jnp.sum(jnp.where(pick, ccol_ref[rows, :], 0.0), axis=-1, keepdims=True)
        logits = logits + cq - crow_ref[:, :n_keys]
        qi = r0 + lax.broadcasted_iota(jnp.int32, (FOX_TQ, n_keys), 0)
        ki = lax.broadcasted_iota(jnp.int32, (FOX_TQ, n_keys), 1)
        logits = jnp.where(qi >= ki, logits, NEG_BIG)
        m = jnp.max(logits, axis=-1, keepdims=True)
        p = jnp.exp(logits - m)
        l = jnp.sum(p, axis=-1, keepdims=True)
        o = _dot(p.astype(BF16), v_ref[:n_keys, :]) / l
        on = _rms(o, ong_ref[...])
        o_ref[rows, :] = (on * _sigmoid(fg_ref[rows, :].astype(F32))).astype(BF16)


def _fox(main, ccol, crow4, gq, gk, ong, batch, seq):
    t = main.shape[0]
    cb = lambda off: off // HEAD_DIM
    return pl.pallas_call(
        _fox_kernel,
        grid=(batch, FOX_HEADS),
        in_specs=[
            pl.BlockSpec((seq, HEAD_DIM), lambda b, h: (b, cb(OFF_FQ) + h)),
            pl.BlockSpec((seq, HEAD_DIM), lambda b, h: (b, cb(OFF_FK) + h)),
            pl.BlockSpec((seq, HEAD_DIM), lambda b, h: (b, cb(OFF_FV) + h)),
            pl.BlockSpec((seq, HEAD_DIM), lambda b, h: (b, cb(OFF_FG) + h)),
            pl.BlockSpec((seq, LANES), lambda b, h: (b, 0)),
            pl.BlockSpec((None, None, 1, seq), lambda b, h: (b, SMALL_FF + h, 0, 0)),
            pl.BlockSpec((1, HEAD_DIM), lambda b, h: (0, 0)),
            pl.BlockSpec((1, HEAD_DIM), lambda b, h: (0, 0)),
            pl.BlockSpec((None, 1, HEAD_DIM), lambda b, h: (h, 0, 0)),
        ],
        out_specs=pl.BlockSpec((seq, HEAD_DIM), lambda b, h: (b, h)),
        out_shape=jax.ShapeDtypeStruct((t, FOX_HEADS * HEAD_DIM), BF16),
        scratch_shapes=[pltpu.VMEM((seq, HEAD_DIM), BF16), pltpu.VMEM((seq, HEAD_DIM), BF16)],
        compiler_params=_params(("parallel", "arbitrary")),
        name="fox",
    )(main, main, main, main, ccol, crow4, gq, gk, ong)


def _memkv_kernel(mem_ref, g_ref, w_ref, gk_ref, mk_ref, mv_ref):
    mn = _rms(mem_ref[...], g_ref[...]).astype(BF16)
    kv = _dot(mn, w_ref[...])
    dk = MEM_HEADS * HEAD_DIM
    for hh in range(MEM_HEADS):
        hl = slice(hh * HEAD_DIM, (hh + 1) * HEAD_DIM)
        mk_ref[:, hl] = _rms(kv[:, hl], gk_ref[...]).astype(BF16)
    mv_ref[...] = kv[:, dk:].astype(BF16)


def _mem_kv(mem2, g, w, gk, batch, n_mem):
    d = mem2.shape[1]
    dk = MEM_HEADS * HEAD_DIM
    return pl.pallas_call(
        _memkv_kernel,
        grid=(batch,),
        in_specs=[
            pl.BlockSpec((n_mem, d), lambda b: (b, 0)),
            pl.BlockSpec((1, d), lambda b: (0, 0)),
            pl.BlockSpec((d, 2 * dk), lambda b: (0, 0)),
            pl.BlockSpec((1, HEAD_DIM), lambda b: (0, 0)),
        ],
        out_specs=[
            pl.BlockSpec((n_mem, dk), lambda b: (b, 0)),
            pl.BlockSpec((n_mem, dk), lambda b: (b, 0)),
        ],
        out_shape=[
            jax.ShapeDtypeStruct((batch * n_mem, dk), BF16),
            jax.ShapeDtypeStruct((batch * n_mem, dk), BF16),
        ],
        compiler_params=_params(("parallel",)),
        name="mem_kv",
    )(mem2, g, w, gk)


def _memattn_kernel(q_ref, mg_ref, mk_ref, mv_ref, gq_ref, ong_ref, o_ref):
    s = q_ref.shape[0]
    scale = HEAD_DIM ** -0.5
    for qb in range(s // MEM_TQ):
        rows = slice(qb * MEM_TQ, (qb + 1) * MEM_TQ)
        qn = (_rms(q_ref[rows, :].astype(F32), gq_ref[...]) * scale).astype(BF16)
        logits = _dot_nt(qn, mk_ref[...])
        m = jnp.max(logits, axis=-1, keepdims=True)
        p = jnp.exp(logits - m)
        l = jnp.sum(p, axis=-1, keepdims=True)
        o = _dot(p.astype(BF16), mv_ref[...]) / l
        on = _rms(o, ong_ref[...])
        o_ref[rows, :] = (on * _sigmoid(mg_ref[rows, :].astype(F32))).astype(BF16)


def _mem_attn(main, mkn, mv, gq, ong, batch, seq, n_mem):
    t = main.shape[0]
    cb = lambda off: off // HEAD_DIM
    return pl.pallas_call(
        _memattn_kernel,
        grid=(batch, MEM_HEADS),
        in_specs=[
            pl.BlockSpec((seq, HEAD_DIM), lambda b, h: (b, cb(OFF_MQ) + h)),
            pl.BlockSpec((seq, HEAD_DIM), lambda b, h: (b, cb(OFF_MG) + h)),
            pl.BlockSpec((n_mem, HEAD_DIM), lambda b, h: (b, h)),
            pl.BlockSpec((n_mem, HEAD_DIM), lambda b, h: (b, h)),
            pl.BlockSpec((1, HEAD_DIM), lambda b, h: (0, 0)),
            pl.BlockSpec((None, 1, HEAD_DIM), lambda b, h: (h, 0, 0)),
        ],
        out_specs=pl.BlockSpec((seq, HEAD_DIM), lambda b, h: (b, h)),
        out_shape=jax.ShapeDtypeStruct((t, MEM_HEADS * HEAD_DIM), BF16),
        compiler_params=_params(("parallel", "arbitrary")),
        name="mem_attn",
    )(main, main, mkn, mv, gq, ong)


def _outproj_kernel(og_ref, of_ref, om_ref, x_ref, wo_ref, gm_ref, h_ref, hn_ref):
    n_g = og_ref.shape[1]
    n_f = of_ref.shape[1]
    acc = _dot(og_ref[...], wo_ref[:n_g, :])
    acc = acc + _dot(of_ref[...], wo_ref[n_g:n_g + n_f, :])
    acc = acc + _dot(om_ref[...], wo_ref[n_g + n_f:, :])
    h = x_ref[...] + acc
    h_ref[...] = h
    hn_ref[...] = _rms(h, gm_ref[...]).astype(BF16)


def _out_proj(og, of, om, x2, wo, gm, tm=512):
    t, d = x2.shape
    row = lambda i: (i, 0)
    fixed = lambda i: (0, 0)
    return pl.pallas_call(
        _outproj_kernel,
        grid=(t // tm,),
        in_specs=[
            pl.BlockSpec((tm, og.shape[1]), row),
            pl.BlockSpec((tm, of.shape[1]), row),
            pl.BlockSpec((tm, om.shape[1]), row),
            pl.BlockSpec((tm, d), row),
            pl.BlockSpec(wo.shape, fixed),
            pl.BlockSpec((1, d), fixed),
        ],
        out_specs=[pl.BlockSpec((tm, d), row), pl.BlockSpec((tm, d), row)],
        out_shape=[jax.ShapeDtypeStruct((t, d), F32), jax.ShapeDtypeStruct((t, d), BF16)],
        compiler_params=_params(("parallel",)),
        name="out_proj",
    )(og, of, om, x2, wo, gm)


def _mlp_kernel(hn_ref, h_ref, wu_ref, wd_ref, o_ref):
    f = pl.program_id(1)
    u = jnp.maximum(_dot(hn_ref[...], wu_ref[...]), 0.0)
    part = _dot((u * u).astype(BF16), wd_ref[...])

    @pl.when(f == 0)
    def _():
        o_ref[...] = h_ref[...] + part

    @pl.when(f > 0)
    def _():
        o_ref[...] += part


def _mlp(hn, h, wu, wd, tm=512, tf=1024):
    t, d = h.shape
    dff = wu.shape[1]
    return pl.pallas_call(
        _mlp_kernel,
        grid=(t // tm, dff // tf),
        in_specs=[
            pl.BlockSpec((tm, d), lambda i, f: (i, 0)),
            pl.BlockSpec((tm, d), lambda i, f: (i, 0)),
            pl.BlockSpec((d, tf), lambda i, f: (0, f)),
            pl.BlockSpec((tf, d), lambda i, f: (f, 0)),
        ],
        out_specs=pl.BlockSpec((tm, d), lambda i, f: (i, 0)),
        out_shape=jax.ShapeDtypeStruct((t, d), F32),
        compiler_params=_params(("parallel", "arbitrary")),
        name="mlp",
    )(hn, h, wu, wd)


def _regroup_w_in(w_in):
    widths = (512, 512, 1024, 1024, GLA_RANK, 512, 512, 512, 512, FOX_HEADS, 512, 512)
    parts, acc = [], 0
    for w in widths:
        parts.append(w_in[:, acc:acc + w])
        acc += w
    gq, gk, gv, gg, ga, fq, fk, fv, fg, ff, mq, mg = parts
    main = jnp.concatenate([gq, gk, gv, gg, fq, fk, fv, fg, mq, mg], axis=1).astype(BF16)
    pad = jnp.zeros((w_in.shape[0], LANES - GLA_RANK - FOX_HEADS), w_in.dtype)
    small = jnp.concatenate([ga, ff, pad], axis=1).astype(BF16)
    return main, small


def kernel(x, mem, attn_norm_g, w_in, gla_a_w2, gla_a_b, fox_f_b, fox_q_norm_g, fox_k_norm_g,
           mem_norm_g, w_mem_kv, mem_q_norm_g, mem_k_norm_g, out_norm_g, w_out, mlp_norm_g,
           w_up, w_down):
    batch, seq, d = x.shape
    n_mem = mem.shape[1]
    t = batch * seq
    x2 = x.reshape(t, d)
    row = lambda a: a.reshape(1, -1).astype(F32)

    w_main, w_small = _regroup_w_in(w_in)
    main, small = _in_proj(x2, row(attn_norm_g), w_main, w_small)

    n_gla = GLA_HEADS * HEAD_DIM
    n_fox = FOX_HEADS * HEAD_DIM
    w2p = jnp.zeros((LANES, GLA_HEADS * GLA_DK), F32).at[:GLA_RANK].set(gla_a_w2).astype(BF16)
    og = _gla(main, small, w2p, row(gla_a_b), row(out_norm_g[:n_gla]), batch, seq)

    fb = jnp.zeros((1, LANES), F32).at[0, SMALL_FF:SMALL_FF + FOX_HEADS].set(fox_f_b)
    ccol, crow = _fox_gate(small, fb, batch, seq)
    ong_fox = out_norm_g[n_gla:n_gla + n_fox].reshape(FOX_HEADS, 1, HEAD_DIM)
    of = _fox(main, ccol, crow.reshape(batch, LANES, 1, seq), row(fox_q_norm_g),
              row(fox_k_norm_g), ong_fox, batch, seq)

    mkn, mv = _mem_kv(mem.reshape(batch * n_mem, d), row(mem_norm_g), w_mem_kv.astype(BF16),
                      row(mem_k_norm_g), batch, n_mem)
    ong_mem = out_norm_g[n_gla + n_fox:].reshape(MEM_HEADS, 1, HEAD_DIM)
    om = _mem_attn(main, mkn, mv, row(mem_q_norm_g), ong_mem, batch, seq, n_mem)

    h, hn = _out_proj(og, of, om, x2, w_out.astype(BF16), row(mlp_norm_g))
    y = _mlp(hn, h, w_up.astype(BF16), w_down.astype(BF16))
    return y.reshape(batch, seq, d)
```

```python
import functools

import jax
import jax.numpy as jnp
from jax import lax
from jax.experimental import pallas as pl
from jax.experimental.pallas import tpu as pltpu

F32 = jnp.float32
BF16 = jnp.bfloat16

HEAD_DIM = 128
GLA_HEADS = 8
GLA_DK = 64
GLA_RANK = 16
GLA_TAU = 16.0
FOX_HEADS = 4
MEM_HEADS = 4
EPS = 1e-6

LANES = 128
SUBLANES = 8
LOG2E = 1.4426950408889634
GLA_CHUNK = 64
GLA_STEP = 128
GLA_SAFE_EXP = 60.0
FOX_TQ = 256
MEM_TQ = 512
NEG_BIG = -1e30
VMEM_LIMIT = 50 * 1024 * 1024

OFF_GQ, OFF_GK, OFF_GV, OFF_GG = 0, 512, 1024, 2048
OFF_FQ, OFF_FK, OFF_FV, OFF_FG = 3072, 3584, 4096, 4608
OFF_MQ, OFF_MG = 5120, 5632
N_MAIN = 6144
SMALL_FF = GLA_RANK
FF_ROW0 = SMALL_FF
assert FF_ROW0 % SUBLANES == 0 and FOX_HEADS <= SUBLANES


def _dot(a, b):
    return jnp.dot(a, b, preferred_element_type=F32)


def _dot_nt(a, b):
    return lax.dot_general(a, b, (((1,), (1,)), ((), ())), preferred_element_type=F32)


def _dot_tn(a, b):
    return lax.dot_general(a, b, (((0,), (0,)), ((), ())), preferred_element_type=F32)


def _rms(x, g):
    return x * lax.rsqrt(jnp.mean(x * x, axis=-1, keepdims=True) + EPS) * g


def _log_sigmoid(z):
    return jnp.minimum(z, 0.0) - jnp.log1p(jnp.exp(-jnp.abs(z)))


def _sigmoid(z):
    return 1.0 / (1.0 + jnp.exp(-z))


def _split3(a):
    hi = a.astype(BF16)
    r = a - hi.astype(F32)
    mid = r.astype(BF16)
    lo = (r - mid.astype(F32)).astype(BF16)
    return hi, mid, lo


def _params(sem):
    return pltpu.CompilerParams(dimension_semantics=sem, vmem_limit_bytes=VMEM_LIMIT)


def _inproj_kernel(x_ref, g_ref, wm_ref, ws_ref, main_ref, small_ref, xn_ref):
    @pl.when(pl.program_id(1) == 0)
    def _():
        xn = _rms(x_ref[...], g_ref[...]).astype(BF16)
        xn_ref[...] = xn
        small_ref[...] = _dot(xn, ws_ref[...])

    main_ref[...] = _dot(xn_ref[...], wm_ref[...]).astype(BF16)


def _in_proj(x2, g, w_main, w_small, tm=1024, tn=1024):
    t, d = x2.shape
    return pl.pallas_call(
        _inproj_kernel,
        grid=(t // tm, N_MAIN // tn),
        in_specs=[
            pl.BlockSpec((tm, d), lambda i, j: (i, 0)),
            pl.BlockSpec((1, d), lambda i, j: (0, 0)),
            pl.BlockSpec((d, tn), lambda i, j: (0, j)),
            pl.BlockSpec((d, LANES), lambda i, j: (0, 0)),
        ],
        out_specs=[
            pl.BlockSpec((tm, tn), lambda i, j: (i, j)),
            pl.BlockSpec((tm, LANES), lambda i, j: (i, 0)),
        ],
        out_shape=[
            jax.ShapeDtypeStruct((t, N_MAIN), BF16),
            jax.ShapeDtypeStruct((t, LANES), F32),
        ],
        scratch_shapes=[pltpu.VMEM((tm, d), BF16)],
        compiler_params=_params(("parallel", "arbitrary")),
        name="in_proj",
    )(x2, g, w_main, w_small)


def _gla_kernel(q_ref, k_ref, v_ref, gg_ref, small_ref, w2_ref, ab_ref, ong_ref,
                o_ref, st_ref, qs_scr, bs_scr, os_scr):
    C = GLA_CHUNK
    n_pairs = GLA_HEADS // 2

    @pl.when(pl.program_id(1) == 0)
    def _():
        st_ref[...] = jnp.zeros_like(st_ref)

    z = _dot(small_ref[...].astype(BF16), w2_ref[...]) + ab_ref[...]
    la = _log_sigmoid(z) * (1.0 / GLA_TAU)

    ri = lax.broadcasted_iota(jnp.int32, (GLA_STEP, GLA_STEP), 0)
    ci = lax.broadcasted_iota(jnp.int32, (GLA_STEP, GLA_STEP), 1)
    same_chunk = (ri ^ ci) < C
    tri = jnp.where(same_chunk, jnp.where(ci <= ri, 1.0, 0.0), 0.0).astype(BF16)
    hi, mid, lo = _split3(la)
    bc = _dot(tri, hi) + _dot(tri, mid) + _dot(tri, lo)

    lane = lax.broadcasted_iota(jnp.int32, (C, LANES), 1)
    low_half = lane < GLA_DK
    st_r = lax.broadcasted_iota(jnp.int32, (2 * HEAD_DIM, LANES), 0)
    st_c = lax.broadcasted_iota(jnp.int32, (2 * HEAD_DIM, LANES), 1)
    st_mask = (st_r < HEAD_DIM) == (st_c < GLA_DK)
    a_r = lax.broadcasted_iota(jnp.int32, (C, 2 * C), 0)
    a_c = lax.broadcasted_iota(jnp.int32, (C, 2 * C), 1)
    causal = a_r >= (a_c & (C - 1))
    zeros_v = jnp.zeros((C, HEAD_DIM), BF16)
    jrow = lax.broadcasted_iota(jnp.int32, (C, LANES), 0)

    def unit(ch, p, fast):
        rows = slice(ch * C, (ch + 1) * C)
        kl = slice(p * LANES, (p + 1) * LANES)
        vl = slice(p * 2 * HEAD_DIM, (p + 1) * 2 * HEAD_DIM)
        b = bc[rows, kl]
        qs = q_ref[rows, kl].astype(F32) * (GLA_DK ** -0.5)
        kf = k_ref[rows, kl].astype(F32)
        v2 = v_ref[rows, vl]
        qe16 = (qs * jnp.exp(b)).astype(BF16)
        b_last = b[C - 1:C, :]
        st = st_ref[p]
        o = _dot_nt(qe16, st.astype(BF16))
        if fast:
            ke = kf * jnp.exp(-b)
            kbd = jnp.concatenate(
                [jnp.where(low_half, ke, 0.0), jnp.where(low_half, 0.0, ke)], axis=0).astype(BF16)
            a = jnp.where(causal, _dot_nt(qe16, kbd), 0.0).astype(BF16)
            vbd = jnp.concatenate(
                [jnp.concatenate([v2[:, :HEAD_DIM], zeros_v], axis=1),
                 jnp.concatenate([zeros_v, v2[:, HEAD_DIM:]], axis=1)], axis=0)
            o = o + _dot(a, vbd)
            upd = _dot_tn(v2, ke.astype(BF16))
            st_ref[p] = jnp.exp(b_last) * (st + jnp.where(st_mask, upd, 0.0))
        else:
            kd16 = (kf * jnp.exp(b_last - b)).astype(BF16)
            upd = _dot_tn(v2, kd16)
            st_ref[p] = jnp.exp(b_last) * st + jnp.where(st_mask, upd, 0.0)
            qs_scr[...] = qs
            bs_scr[...] = b
            v2f = v2.astype(F32)

            def row(i, carry):
                qi = qs_scr[pl.ds(i, 1), :]
                bi = bs_scr[pl.ds(i, 1), :]
                w = qi * kf * jnp.exp(jnp.minimum(bi - b, 0.0))
                w = jnp.where(jrow <= i, w, 0.0)
                p0 = jnp.sum(jnp.where(low_half, w, 0.0), axis=1, keepdims=True)
                p1 = jnp.sum(jnp.where(low_half, 0.0, w), axis=1, keepdims=True)
                pv = jnp.concatenate([p0 * v2f[:, :HEAD_DIM], p1 * v2f[:, HEAD_DIM:]], axis=1)
                os_scr[pl.ds(i, 1), :] = jnp.sum(pv, axis=0, keepdims=True)
                return carry

            lax.fori_loop(0, C, row, 0)
            o = o + os_scr[...]
        for hh in range(2):
            hl = slice((2 * p + hh) * HEAD_DIM, (2 * p + hh + 1) * HEAD_DIM)
            on = _rms(o[:, hh * HEAD_DIM:(hh + 1) * HEAD_DIM], ong_ref[:, hl])
            g = gg_ref[rows, hl].astype(F32)
            o_ref[rows, hl] = (on * (g * _sigmoid(g))).astype(BF16)

    def run(fast):
        for ch in range(GLA_STEP // C):
            for p in range(n_pairs):
                unit(ch, p, fast)

    safe = jnp.min(la) * C >= -GLA_SAFE_EXP

    @pl.when(safe)
    def _():
        run(True)

    @pl.when(jnp.logical_not(safe))
    def _():
        run(False)


def _gla(main, small, w2p, ab, ong, batch, seq):
    t = main.shape[0]
    nc = seq // GLA_STEP
    rb = lambda b, c: b * nc + c
    dq = GLA_HEADS * GLA_DK
    dv = GLA_HEADS * HEAD_DIM
    return pl.pallas_call(
        _gla_kernel,
        grid=(batch, nc),
        in_specs=[
            pl.BlockSpec((GLA_STEP, dq), lambda b, c: (rb(b, c), OFF_GQ // dq)),
            pl.BlockSpec((GLA_STEP, dq), lambda b, c: (rb(b, c), OFF_GK // dq)),
            pl.BlockSpec((GLA_STEP, dv), lambda b, c: (rb(b, c), OFF_GV // dv)),
            pl.BlockSpec((GLA_STEP, dv), lambda b, c: (rb(b, c), OFF_GG // dv)),
            pl.BlockSpec((GLA_STEP, LANES), lambda b, c: (rb(b, c), 0)),
            pl.BlockSpec((LANES, dq), lambda b, c: (0, 0)),
            pl.BlockSpec((1, dq), lambda b, c: (0, 0)),
            pl.BlockSpec((1, dv), lambda b, c: (0, 0)),
        ],
        out_specs=pl.BlockSpec((GLA_STEP, dv), lambda b, c: (rb(b, c), 0)),
        out_shape=jax.ShapeDtypeStruct((t, dv), BF16),
        scratch_shapes=[
            pltpu.VMEM((GLA_HEADS // 2, 2 * HEAD_DIM, LANES), F32),
            pltpu.VMEM((GLA_CHUNK, LANES), F32),
            pltpu.VMEM((GLA_CHUNK, LANES), F32),
            pltpu.VMEM((GLA_CHUNK, 2 * HEAD_DIM), F32),
        ],
        compiler_params=_params(("parallel", "arbitrary")),
        name="gla",
    )(main, main, main, main, small, w2p, ab, ong)


def _foxgate_kernel(small_ref, fb_ref, crow_ref, c_scr):
    s = small_ref.shape[0]
    blk = 256
    ri = lax.broadcasted_iota(jnp.int32, (blk, blk), 0)
    ci = lax.broadcasted_iota(jnp.int32, (blk, blk), 1)
    tri = jnp.where(ci <= ri, 1.0, 0.0).astype(BF16)
    carry = jnp.zeros((1, LANES), F32)
    for i in range(s // blk):
        rows = slice(i * blk, (i + 1) * blk)
        lf = _log_sigmoid(small_ref[rows, :] + fb_ref[...]) * LOG2E
        hi, mid, lo = _split3(lf)
        c = _dot(tri, hi) + _dot(tri, mid) + _dot(tri, lo) + carry
        c_scr[rows, :] = c
        carry = c[blk - 1:blk, :]
    crow_ref[...] = c_scr[...].T[FF_ROW0:FF_ROW0 + SUBLANES, :]


def _fox_gate(small, fb, batch, seq):
    return pl.pallas_call(
        _foxgate_kernel,
        grid=(batch,),
        in_specs=[
            pl.BlockSpec((seq, LANES), lambda b: (b, 0)),
            pl.BlockSpec((1, LANES), lambda b: (0, 0)),
        ],
        out_specs=pl.BlockSpec((None, SUBLANES, seq), lambda b: (b, 0, 0)),
        out_shape=jax.ShapeDtypeStruct((batch, SUBLANES, seq), F32),
        scratch_shapes=[pltpu.VMEM((seq, LANES), F32)],
        compiler_params=_params(("parallel",)),
        name="fox_gate",
    )(small, fb)


def _fox_kernel(q_ref, k_ref, v_ref, fg_ref, crow_ref, gq_ref, gk_ref, ong_ref,
                o_ref, qn_scr, kn_scr, v1_scr):
    s = q_ref.shape[0]
    scale = HEAD_DIM ** -0.5 * LOG2E
    qn_scr[...] = (_rms(q_ref[...].astype(F32), gq_ref[...]) * scale).astype(BF16)
    kn_scr[...] = _rms(k_ref[...].astype(F32), gk_ref[...]).astype(BF16)
    v1_scr[:, :HEAD_DIM] = v_ref[...]
    v1_scr[:, HEAD_DIM:] = jnp.ones((s, HEAD_DIM), BF16)
    qi = lax.broadcasted_iota(jnp.int32, (FOX_TQ, FOX_TQ), 0)
    ki = lax.broadcasted_iota(jnp.int32, (FOX_TQ, FOX_TQ), 1)
    causal = qi >= ki
    for qb in range(s // FOX_TQ):
        r0 = qb * FOX_TQ
        rows = slice(r0, r0 + FOX_TQ)
        qn = qn_scr[rows, :]
        diag = _dot_nt(qn, kn_scr[rows, :]) - crow_ref[:, rows]
        diag = jnp.where(causal, diag, NEG_BIG)
        m = jnp.max(diag, axis=-1, keepdims=True)
        if r0:
            past = _dot_nt(qn, kn_scr[:r0, :]) - crow_ref[:, :r0]
            m = jnp.maximum(m, jnp.max(past, axis=-1, keepdims=True))
            ol = _dot(jnp.exp2(past - m).astype(BF16), v1_scr[:r0, :])
            ol = ol + _dot(jnp.exp2(diag - m).astype(BF16), v1_scr[rows, :])
        else:
            ol = _dot(jnp.exp2(diag - m).astype(BF16), v1_scr[rows, :])
        o = ol[:, :HEAD_DIM] / ol[:, HEAD_DIM:]
        on = _rms(o, ong_ref[...])
        o_ref[rows, :] = (on * _sigmoid(fg_ref[rows, :].astype(F32))).astype(BF16)


def _fox(main, crow4, gq, gk, ong, batch, seq):
    t = main.shape[0]
    cb = lambda off: off // HEAD_DIM
    return pl.pallas_call(
        _fox_kernel,
        grid=(batch, FOX_HEADS),
        in_specs=[
            pl.BlockSpec((seq, HEAD_DIM), lambda b, h: (b, cb(OFF_FQ) + h)),
            pl.BlockSpec((seq, HEAD_DIM), lambda b, h: (b, cb(OFF_FK) + h)),
            pl.BlockSpec((seq, HEAD_DIM), lambda b, h: (b, cb(OFF_FV) + h)),
            pl.BlockSpec((seq, HEAD_DIM), lambda b, h: (b, cb(OFF_FG) + h)),
            pl.BlockSpec((None, None, 1, seq), lambda b, h: (b, h, 0, 0)),
            pl.BlockSpec((1, HEAD_DIM), lambda b, h: (0, 0)),
            pl.BlockSpec((1, HEAD_DIM), lambda b, h: (0, 0)),
            pl.BlockSpec((None, 1, HEAD_DIM), lambda b, h: (h, 0, 0)),
        ],
        out_specs=pl.BlockSpec((seq, HEAD_DIM), lambda b, h: (b, h)),
        out_shape=jax.ShapeDtypeStruct((t, FOX_HEADS * HEAD_DIM), BF16),
        scratch_shapes=[
            pltpu.VMEM((seq, HEAD_DIM), BF16),
            pltpu.VMEM((seq, HEAD_DIM), BF16),
            pltpu.VMEM((seq, 2 * HEAD_DIM), BF16),
        ],
        compiler_params=_params(("parallel", "arbitrary")),
        name="fox",
    )(main, main, main, main, crow4, gq, gk, ong)


def _memkv_kernel(mem_ref, g_ref, w_ref, gk_ref, mk_ref, mv_ref):
    mn = _rms(mem_ref[...], g_ref[...]).astype(BF16)
    kv = _dot(mn, w_ref[...])
    dk = MEM_HEADS * HEAD_DIM
    ones = jnp.ones((kv.shape[0], HEAD_DIM), BF16)
    for hh in range(MEM_HEADS):
        hl = slice(hh * HEAD_DIM, (hh + 1) * HEAD_DIM)
        mk_ref[:, hl] = _rms(kv[:, hl], gk_ref[...]).astype(BF16)
        mv_ref[:, 2 * hh * HEAD_DIM:(2 * hh + 1) * HEAD_DIM] = kv[:, dk + hh * HEAD_DIM:dk + (hh + 1) * HEAD_DIM].astype(BF16)
        mv_ref[:, (2 * hh + 1) * HEAD_DIM:(2 * hh + 2) * HEAD_DIM] = ones


def _mem_kv(mem2, g, w, gk, batch, n_mem):
    d = mem2.shape[1]
    dk = MEM_HEADS * HEAD_DIM
    return pl.pallas_call(
        _memkv_kernel,
        grid=(batch,),
        in_specs=[
            pl.BlockSpec((n_mem, d), lambda b: (b, 0)),
            pl.BlockSpec((1, d), lambda b: (0, 0)),
            pl.BlockSpec((d, 2 * dk), lambda b: (0, 0)),
            pl.BlockSpec((1, HEAD_DIM), lambda b: (0, 0)),
        ],
        out_specs=[
            pl.BlockSpec((n_mem, dk), lambda b: (b, 0)),
            pl.BlockSpec((n_mem, 2 * dk), lambda b: (b, 0)),
        ],
        out_shape=[
            jax.ShapeDtypeStruct((batch * n_mem, dk), BF16),
            jax.ShapeDtypeStruct((batch * n_mem, 2 * dk), BF16),
        ],
        compiler_params=_params(("parallel",)),
        name="mem_kv",
    )(mem2, g, w, gk)


def _memattn_kernel(q_ref, mg_ref, mk_ref, mv_ref, gq_ref, ong_ref, o_ref):
    s = q_ref.shape[0]
    scale = HEAD_DIM ** -0.5 * LOG2E
    for qb in range(s // MEM_TQ):
        rows = slice(qb * MEM_TQ, (qb + 1) * MEM_TQ)
        qn = (_rms(q_ref[rows, :].astype(F32), gq_ref[...]) * scale).astype(BF16)
        logits = _dot_nt(qn, mk_ref[...])
        m = jnp.max(logits, axis=-1, keepdims=True)
        ol = _dot(jnp.exp2(logits - m).astype(BF16), mv_ref[...])
        o = ol[:, :HEAD_DIM] / ol[:, HEAD_DIM:]
        on = _rms(o, ong_ref[...])
        o_ref[rows, :] = (on * _sigmoid(mg_ref[rows, :].astype(F32))).astype(BF16)


def _mem_attn(main, mkn, mv, gq, ong, batch, seq, n_mem):
    t = main.shape[0]
    cb = lambda off: off // HEAD_DIM
    return pl.pallas_call(
        _memattn_kernel,
        grid=(batch, MEM_HEADS),
        in_specs=[
            pl.BlockSpec((seq, HEAD_DIM), lambda b, h: (b, cb(OFF_MQ) + h)),
            pl.BlockSpec((seq, HEAD_DIM), lambda b, h: (b, cb(OFF_MG) + h)),
            pl.BlockSpec((n_mem, HEAD_DIM), lambda b, h: (b, h)),
            pl.BlockSpec((n_mem, 2 * HEAD_DIM), lambda b, h: (b, h)),
            pl.BlockSpec((1, HEAD_DIM), lambda b, h: (0, 0)),
            pl.BlockSpec((None, 1, HEAD_DIM), lambda b, h: (h, 0, 0)),
        ],
        out_specs=pl.BlockSpec((seq, HEAD_DIM), lambda b, h: (b, h)),
        out_shape=jax.ShapeDtypeStruct((t, MEM_HEADS * HEAD_DIM), BF16),
        compiler_params=_params(("parallel", "arbitrary")),
        name="mem_attn",
    )(main, main, mkn, mv, gq, ong)


def _outproj_kernel(og_ref, of_ref, om_ref, x_ref, wo_ref, gm_ref, h_ref, hn_ref):
    n_g = og_ref.shape[1]
    n_f = of_ref.shape[1]
    acc = _dot(og_ref[...], wo_ref[:n_g, :])
    acc = acc + _dot(of_ref[...], wo_ref[n_g:n_g + n_f, :])
    acc = acc + _dot(om_ref[...], wo_ref[n_g + n_f:, :])
    h = x_ref[...] + acc
    h_ref[...] = h
    hn_ref[...] = _rms(h, gm_ref[...]).astype(BF16)


def _out_proj(og, of, om, x2, wo, gm, tm=512):
    t, d = x2.shape
    row = lambda i: (i, 0)
    fixed = lambda i: (0, 0)
    return pl.pallas_call(
        _outproj_kernel,
        grid=(t // tm,),
        in_specs=[
            pl.BlockSpec((tm, og.shape[1]), row),
            pl.BlockSpec((tm, of.shape[1]), row),
            pl.BlockSpec((tm, om.shape[1]), row),
            pl.BlockSpec((tm, d), row),
            pl.BlockSpec(wo.shape, fixed),
            pl.BlockSpec((1, d), fixed),
        ],
        out_specs=[pl.BlockSpec((tm, d), row), pl.BlockSpec((tm, d), row)],
        out_shape=[jax.ShapeDtypeStruct((t, d), F32), jax.ShapeDtypeStruct((t, d), BF16)],
        compiler_params=_params(("parallel",)),
        name="out_proj",
    )(og, of, om, x2, wo, gm)


def _mlp_kernel(hn_ref, h_ref, wu_ref, wd_ref, o_ref):
    @pl.when(pl.program_id(1) == 0)
    def _():
        o_ref[...] = h_ref[...]

    u = jnp.maximum(_dot(hn_ref[...], wu_ref[...]), 0.0)
    o_ref[...] += _dot((u * u).astype(BF16), wd_ref[...])


def _mlp(hn, h, wu, wd, tm=512, tf=1024):
    t, d = h.shape
    dff = wu.shape[1]
    return pl.pallas_call(
        _mlp_kernel,
        grid=(t // tm, dff // tf),
        in_specs=[
            pl.BlockSpec((tm, d), lambda i, f: (i, 0)),
            pl.BlockSpec((tm, d), lambda i, f: (i, 0)),
            pl.BlockSpec((d, tf), lambda i, f: (0, f)),
            pl.BlockSpec((tf, d), lambda i, f: (f, 0)),
        ],
        out_specs=pl.BlockSpec((tm, d), lambda i, f: (i, 0)),
        out_shape=jax.ShapeDtypeStruct((t, d), F32),
        compiler_params=_params(("parallel", "arbitrary")),
        name="mlp",
    )(hn, h, wu, wd)


def _regroup_w_in(w_in):
    n_gla = OFF_FQ
    n_fox = OFF_MQ - OFF_FQ
    a0 = n_gla
    f0 = a0 + GLA_RANK
    ff0 = f0 + n_fox
    m0 = ff0 + FOX_HEADS
    wb = w_in.astype(BF16)
    main = jnp.concatenate([wb[:, :n_gla], wb[:, f0:ff0], wb[:, m0:]], axis=1)
    pad = jnp.zeros((w_in.shape[0], LANES - GLA_RANK - FOX_HEADS), BF16)
    small = jnp.concatenate([wb[:, a0:f0], wb[:, ff0:m0], pad], axis=1)
    return main, small


def kernel(x, mem, attn_norm_g, w_in, gla_a_w2, gla_a_b, fox_f_b, fox_q_norm_g, fox_k_norm_g,
           mem_norm_g, w_mem_kv, mem_q_norm_g, mem_k_norm_g, out_norm_g, w_out, mlp_norm_g,
           w_up, w_down):
    batch, seq, d = x.shape
    n_mem = mem.shape[1]
    t = batch * seq
    x2 = x.reshape(t, d)
    row = lambda a: a.reshape(1, -1).astype(F32)

    w_main, w_small = _regroup_w_in(w_in)
    main, small = _in_proj(x2, row(attn_norm_g), w_main, w_small)

    n_gla = GLA_HEADS * HEAD_DIM
    n_fox = FOX_HEADS * HEAD_DIM
    w2p = jnp.zeros((LANES, GLA_HEADS * GLA_DK), F32).at[:GLA_RANK].set(gla_a_w2).astype(BF16)
    og = _gla(main, small, w2p, row(gla_a_b), row(out_norm_g[:n_gla]), batch, seq)

    fb = jnp.zeros((1, LANES), F32).at[0, SMALL_FF:SMALL_FF + FOX_HEADS].set(fox_f_b)
    crow = _fox_gate(small, fb, batch, seq)
    ong_fox = out_norm_g[n_gla:n_gla + n_fox].reshape(FOX_HEADS, 1, HEAD_DIM)
    of = _fox(main, crow.reshape(batch, SUBLANES, 1, seq), row(fox_q_norm_g),
              row(fox_k_norm_g), ong_fox, batch, seq)

    mkn, mv = _mem_kv(mem.reshape(batch * n_mem, d), row(mem_norm_g), w_mem_kv.astype(BF16),
                      row(mem_k_norm_g), batch, n_mem)
    ong_mem = out_norm_g[n_gla + n_fox:].reshape(MEM_HEADS, 1, HEAD_DIM)
    om = _mem_attn(main, mkn, mv, row(mem_q_norm_g), ong_mem, batch, seq, n_mem)

    h, hn = _out_proj(og, of, om, x2, w_out.astype(BF16), row(mlp_norm_g))
    y = _mlp(hn, h, w_up.astype(BF16), w_down.astype(BF16))
    return y.reshape(batch, seq, d)
```

```python
import functools

import jax
import jax.numpy as jnp
import numpy as np
from jax import lax
from jax.experimental import pallas as pl
from jax.experimental.pallas import tpu as pltpu

F32 = jnp.float32
BF16 = jnp.bfloat16

HEAD_DIM = 128
GLA_HEADS = 8
GLA_DK = 64
GLA_RANK = 16
GLA_TAU = 16.0
FOX_HEADS = 4
MEM_HEADS = 4
EPS = 1e-6

LANES = 128
SUBLANES = 8
LOG2E = 1.4426950408889634
GLA_CHUNK = 64
GLA_STEP = 512
GLA_SAFE_EXP = 60.0
FOX_TQ = 256
FOX_AUG = 8
FOX_MAX_SPAN = 100.0
MEM_TQ = 512
NEG_BIG = -1e30
VMEM_LIMIT = 50 * 1024 * 1024

OFF_GQ, OFF_GK, OFF_GV, OFF_GG = 0, 512, 1024, 2048
OFF_FQ, OFF_FK, OFF_FV, OFF_FG = 3072, 3584, 4096, 4608
OFF_MQ, OFF_MG = 5120, 5632
N_MAIN = 6144
SMALL_FF = GLA_RANK
FF_ROW0 = SMALL_FF
assert FF_ROW0 % SUBLANES == 0 and FOX_HEADS <= SUBLANES


def _dot(a, b):
    return jnp.dot(a, b, preferred_element_type=F32)


def _dot_nt(a, b):
    return lax.dot_general(a, b, (((1,), (1,)), ((), ())), preferred_element_type=F32)


def _dot_tn(a, b):
    return lax.dot_general(a, b, (((0,), (0,)), ((), ())), preferred_element_type=F32)


def _rms(x, g):
    return x * lax.rsqrt(jnp.mean(x * x, axis=-1, keepdims=True) + EPS) * g


def _log_sigmoid(z):
    return jnp.minimum(z, 0.0) - jnp.log(1.0 + jnp.exp(-jnp.abs(z)))


def _sigmoid(z):
    return 1.0 / (1.0 + jnp.exp(-z))


def _split3(a):
    hi = a.astype(BF16)
    r = a - hi.astype(F32)
    mid = r.astype(BF16)
    lo = (r - mid.astype(F32)).astype(BF16)
    return hi, mid, lo


def _params(sem):
    return pltpu.CompilerParams(dimension_semantics=sem, vmem_limit_bytes=VMEM_LIMIT)


def _inproj_kernel(x_ref, g_ref, wm_ref, ws_ref, main_ref, small_ref, xn_ref):
    @pl.when(pl.program_id(1) == 0)
    def _():
        xn = _rms(x_ref[...], g_ref[...]).astype(BF16)
        xn_ref[...] = xn
        small_ref[...] = _dot(xn, ws_ref[...])

    main_ref[...] = _dot(xn_ref[...], wm_ref[...]).astype(BF16)


def _in_proj(x2, g, w_main, w_small, tm=1024, tn=1024):
    t, d = x2.shape
    return pl.pallas_call(
        _inproj_kernel,
        grid=(t // tm, N_MAIN // tn),
        in_specs=[
            pl.BlockSpec((tm, d), lambda i, j: (i, 0)),
            pl.BlockSpec((1, d), lambda i, j: (0, 0)),
            pl.BlockSpec((d, tn), lambda i, j: (0, j)),
            pl.BlockSpec((d, LANES), lambda i, j: (0, 0)),
        ],
        out_specs=[
            pl.BlockSpec((tm, tn), lambda i, j: (i, j)),
            pl.BlockSpec((tm, LANES), lambda i, j: (i, 0)),
        ],
        out_shape=[
            jax.ShapeDtypeStruct((t, N_MAIN), BF16),
            jax.ShapeDtypeStruct((t, LANES), F32),
        ],
        scratch_shapes=[pltpu.VMEM((tm, d), BF16)],
        compiler_params=_params(("parallel", "arbitrary")),
        name="in_proj",
    )(x2, g, w_main, w_small)


def _gla_kernel(q_ref, k_ref, v_ref, gg_ref, small_ref, small_next_ref, w2_ref, ab_ref, ong_ref,
                o_ref, st_ref, bc_scr, safe_ref, qs_scr, bs_scr, os_scr):
    C = GLA_CHUNK
    n_pairs = GLA_HEADS // 2
    step = pl.program_id(0) * pl.num_programs(1) + pl.program_id(1)
    slot = step % 2

    @pl.when(pl.program_id(1) == 0)
    def _():
        st_ref[...] = jnp.zeros_like(st_ref)

    def decay_sums(src_ref, dst):
        z = _dot(src_ref[...].astype(BF16), w2_ref[...]) + ab_ref[...]
        la = _log_sigmoid(z) * (1.0 / GLA_TAU)
        ri = lax.broadcasted_iota(jnp.int32, (GLA_STEP, GLA_STEP), 0)
        ci = lax.broadcasted_iota(jnp.int32, (GLA_STEP, GLA_STEP), 1)
        same_chunk = (ri ^ ci) < C
        tri = jnp.where(same_chunk, jnp.where(ci <= ri, 1.0, 0.0), 0.0).astype(BF16)
        hi, mid, lo = _split3(la)
        bc_scr[dst] = _dot(tri, hi) + _dot(tri, mid) + _dot(tri, lo)
        safe_ref[dst] = (jnp.min(la) * C >= -GLA_SAFE_EXP).astype(jnp.int32)

    @pl.when(step == 0)
    def _():
        decay_sums(small_ref, 0)

    lane = lax.broadcasted_iota(jnp.int32, (C, LANES), 1)
    low_half = lane < GLA_DK
    st_r = lax.broadcasted_iota(jnp.int32, (2 * HEAD_DIM, LANES), 0)
    st_c = lax.broadcasted_iota(jnp.int32, (2 * HEAD_DIM, LANES), 1)
    st_mask = (st_r < HEAD_DIM) == (st_c < GLA_DK)
    a_r = lax.broadcasted_iota(jnp.int32, (C, 2 * C), 0)
    a_c = lax.broadcasted_iota(jnp.int32, (C, 2 * C), 1)
    causal = a_r >= (a_c & (C - 1))
    zeros_v = jnp.zeros((C, HEAD_DIM), BF16)
    jrow = lax.broadcasted_iota(jnp.int32, (C, LANES), 0)

    def unit(ch, p, fast):
        rows = slice(ch * C, (ch + 1) * C)
        kl = slice(p * LANES, (p + 1) * LANES)
        vl = slice(p * 2 * HEAD_DIM, (p + 1) * 2 * HEAD_DIM)
        b = bc_scr[slot, rows, kl]
        qs = q_ref[rows, kl].astype(F32) * (GLA_DK ** -0.5)
        kf = k_ref[rows, kl].astype(F32)
        v2 = v_ref[rows, vl]
        qe16 = (qs * jnp.exp(b)).astype(BF16)
        b_last = b[C - 1:C, :]
        st = st_ref[p]
        o = _dot_nt(qe16, st.astype(BF16))
        if fast:
            ke = kf * jnp.exp(-b)
            kbd = jnp.concatenate(
                [jnp.where(low_half, ke, 0.0), jnp.where(low_half, 0.0, ke)], axis=0).astype(BF16)
            a = jnp.where(causal, _dot_nt(qe16, kbd), 0.0).astype(BF16)
            vbd = jnp.concatenate(
                [jnp.concatenate([v2[:, :HEAD_DIM], zeros_v], axis=1),
                 jnp.concatenate([zeros_v, v2[:, HEAD_DIM:]], axis=1)], axis=0)
            o = o + _dot(a, vbd)
            st_ref[p] = jnp.exp(b_last) * (st + _dot_tn(vbd, kbd))
        else:
            kd16 = (kf * jnp.exp(b_last - b)).astype(BF16)
            upd = _dot_tn(v2, kd16)
            st_ref[p] = jnp.exp(b_last) * st + jnp.where(st_mask, upd, 0.0)
            qs_scr[...] = qs
            bs_scr[...] = b
            v2f = v2.astype(F32)

            def row(i, carry):
                qi = qs_scr[pl.ds(i, 1), :]
                bi = bs_scr[pl.ds(i, 1), :]
                w = qi * kf * jnp.exp(jnp.minimum(bi - b, 0.0))
                w = jnp.where(jrow <= i, w, 0.0)
                p0 = jnp.sum(jnp.where(low_half, w, 0.0), axis=1, keepdims=True)
                p1 = jnp.sum(jnp.where(low_half, 0.0, w), axis=1, keepdims=True)
                pv = jnp.concatenate([p0 * v2f[:, :HEAD_DIM], p1 * v2f[:, HEAD_DIM:]], axis=1)
                os_scr[pl.ds(i, 1), :] = jnp.sum(pv, axis=0, keepdims=True)
                return carry

            lax.fori_loop(0, C, row, 0)
            o = o + os_scr[...]
        for hh in range(2):
            hl = slice((2 * p + hh) * HEAD_DIM, (2 * p + hh + 1) * HEAD_DIM)
            on = _rms(o[:, hh * HEAD_DIM:(hh + 1) * HEAD_DIM], ong_ref[:, hl])
            g = gg_ref[rows, hl].astype(F32)
            o_ref[rows, hl] = (on * (g * _sigmoid(g))).astype(BF16)

    def run(fast):
        decay_sums(small_next_ref, 1 - slot)
        for ch in range(GLA_STEP // C):
            for p in range(n_pairs):
                unit(ch, p, fast)

    safe = safe_ref[slot]

    @pl.when(safe == 1)
    def _():
        run(True)

    @pl.when(safe == 0)
    def _():
        run(False)


def _gla(main, small, w2p, ab, ong, batch, seq):
    t = main.shape[0]
    nc = seq // GLA_STEP
    rb = lambda b, c: b * nc + c
    rb_next = lambda b, c: jnp.minimum(b * nc + c + 1, batch * nc - 1)
    dq = GLA_HEADS * GLA_DK
    dv = GLA_HEADS * HEAD_DIM
    return pl.pallas_call(
        _gla_kernel,
        grid=(batch, nc),
        in_specs=[
            pl.BlockSpec((GLA_STEP, dq), lambda b, c: (rb(b, c), OFF_GQ // dq)),
            pl.BlockSpec((GLA_STEP, dq), lambda b, c: (rb(b, c), OFF_GK // dq)),
            pl.BlockSpec((GLA_STEP, dv), lambda b, c: (rb(b, c), OFF_GV // dv)),
            pl.BlockSpec((GLA_STEP, dv), lambda b, c: (rb(b, c), OFF_GG // dv)),
            pl.BlockSpec((GLA_STEP, LANES), lambda b, c: (rb(b, c), 0)),
            pl.BlockSpec((GLA_STEP, LANES), lambda b, c: (rb_next(b, c), 0)),
            pl.BlockSpec((LANES, dq), lambda b, c: (0, 0)),
            pl.BlockSpec((1, dq), lambda b, c: (0, 0)),
            pl.BlockSpec((1, dv), lambda b, c: (0, 0)),
        ],
        out_specs=pl.BlockSpec((GLA_STEP, dv), lambda b, c: (rb(b, c), 0)),
        out_shape=jax.ShapeDtypeStruct((t, dv), BF16),
        scratch_shapes=[
            pltpu.VMEM((GLA_HEADS // 2, 2 * HEAD_DIM, LANES), F32),
            pltpu.VMEM((2, GLA_STEP, dq), F32),
            pltpu.SMEM((2,), jnp.int32),
            pltpu.VMEM((GLA_CHUNK, LANES), F32),
            pltpu.VMEM((GLA_CHUNK, LANES), F32),
            pltpu.VMEM((GLA_CHUNK, 2 * HEAD_DIM), F32),
        ],
        compiler_params=_params(("arbitrary", "arbitrary")),
        name="gla",
    )(main, main, main, main, small, small, w2p, ab, ong)


def _foxgate_kernel(small_ref, fb_ref, gq_ref, gk_ref, p_ref, rows_ref,
                    crow_ref, aq_ref, ak_ref, c_scr):
    s = small_ref.shape[0]
    blk = 256
    ri = lax.broadcasted_iota(jnp.int32, (blk, blk), 0)
    ci = lax.broadcasted_iota(jnp.int32, (blk, blk), 1)
    tri = jnp.where(ci <= ri, 1.0, 0.0).astype(BF16)
    carry = jnp.zeros((1, LANES), F32)
    qk_bound = _fox_qk_bound(gq_ref[...], gk_ref[...])
    b_hi = qk_bound.astype(BF16).astype(F32)
    b_lo = (qk_bound - b_hi).astype(BF16).astype(F32)
    q_const = rows_ref[0:1, :]
    k_const = rows_ref[1:2, :] + b_hi * rows_ref[2:3, :] + b_lo * rows_ref[3:4, :]
    for i in range(s // blk):
        rows = slice(i * blk, (i + 1) * blk)
        lf = _log_sigmoid(small_ref[rows, :] + fb_ref[...]) * LOG2E
        hi, mid, lo = _split3(lf)
        c = _dot(tri, hi) + _dot(tri, mid) + _dot(tri, lo) + carry
        c_scr[rows, :] = c
        carry = c[blk - 1:blk, :]
        aug = _dot(jnp.concatenate(_split3(c), axis=1), p_ref[...])
        aq_ref[rows, :] = (aug[:, :LANES] + q_const).astype(BF16)
        ak_ref[rows, :] = (aug[:, LANES:] + k_const).astype(BF16)
    crow_ref[...] = c_scr[...].T[FF_ROW0:FF_ROW0 + SUBLANES, :]


def _fox_qk_bound(gq, gk):
    gmax = jnp.max(jnp.abs(gq), axis=-1, keepdims=True) * jnp.max(jnp.abs(gk), axis=-1, keepdims=True)
    return gmax * (1.01 * LOG2E * HEAD_DIM ** 0.5)


def _fox_aug_constants():
    p = np.zeros((3 * LANES, 2 * LANES), np.float32)
    rows = np.zeros((SUBLANES, LANES), np.float32)
    for h in range(FOX_HEADS):
        base = FOX_AUG * h
        for j in range(3):
            p[j * LANES + SMALL_FF + h, base + j] = 1.0
            p[j * LANES + SMALL_FF + h, LANES + base + 3 + j] = -1.0
            rows[1, base + j] = 1.0
        rows[0, base + 3:base + 8] = 1.0
        rows[2, base + 6] = -1.0
        rows[3, base + 7] = -1.0
    return jnp.asarray(p, BF16), jnp.asarray(rows)


def _fox_gate(small, fb, gq, gk, batch, seq):
    t = small.shape[0]
    p, const_rows = _fox_aug_constants()
    fixed = lambda b: (0, 0)
    return pl.pallas_call(
        _foxgate_kernel,
        grid=(batch,),
        in_specs=[
            pl.BlockSpec((seq, LANES), lambda b: (b, 0)),
            pl.BlockSpec((1, LANES), fixed),
            pl.BlockSpec((1, HEAD_DIM), fixed),
            pl.BlockSpec((1, HEAD_DIM), fixed),
            pl.BlockSpec(p.shape, fixed),
            pl.BlockSpec(const_rows.shape, fixed),
        ],
        out_specs=[
            pl.BlockSpec((None, SUBLANES, seq), lambda b: (b, 0, 0)),
            pl.BlockSpec((seq, LANES), lambda b: (b, 0)),
            pl.BlockSpec((seq, LANES), lambda b: (b, 0)),
        ],
        out_shape=[
            jax.ShapeDtypeStruct((batch, SUBLANES, seq), F32),
            jax.ShapeDtypeStruct((t, LANES), BF16),
            jax.ShapeDtypeStruct((t, LANES), BF16),
        ],
        scratch_shapes=[pltpu.VMEM((seq, LANES), F32)],
        compiler_params=_params(("parallel",)),
        name="fox_gate",
    )(small, fb, gq, gk, p, const_rows)


def _fox_kernel(q_ref, k_ref, v_ref, fg_ref, aq_ref, ak_ref, crow_ref, gq_ref, gk_ref, ong_ref,
                o_ref, ka_scr, v1_scr):
    s = q_ref.shape[0]
    h = pl.program_id(1)
    scale = HEAD_DIM ** -0.5 * LOG2E
    lane = lax.broadcasted_iota(jnp.int32, (FOX_TQ, LANES), 1)
    own = (lane >= FOX_AUG * h) & (lane < FOX_AUG * (h + 1))
    qi = lax.broadcasted_iota(jnp.int32, (FOX_TQ, FOX_TQ), 0)
    ki = lax.broadcasted_iota(jnp.int32, (FOX_TQ, FOX_TQ), 1)
    causal = qi >= ki
    ones = jnp.ones((FOX_TQ, HEAD_DIM), BF16)

    def prepare(rows):
        qn = (_rms(q_ref[rows, :].astype(F32), gq_ref[...]) * scale).astype(BF16)
        kn = _rms(k_ref[rows, :].astype(F32), gk_ref[...]).astype(BF16)
        qa = jnp.concatenate(
            [qn, jnp.where(own, aq_ref[rows, :].astype(F32), 0.0).astype(BF16)], axis=1)
        ka = jnp.concatenate([kn, ak_ref[rows, :]], axis=1)
        v1 = jnp.concatenate([v_ref[rows, :], ones], axis=1)
        ka_scr[rows, :] = ka
        v1_scr[rows, :] = v1
        return qa, ka, v1

    def finish(rows, ol):
        o = ol[:, :HEAD_DIM] / ol[:, HEAD_DIM:]
        on = _rms(o, ong_ref[...])
        o_ref[rows, :] = (on * _sigmoid(fg_ref[rows, :].astype(F32))).astype(BF16)

    bounded = 2.0 * _fox_qk_bound(gq_ref[...], gk_ref[...])[0, 0] <= FOX_MAX_SPAN

    @pl.when(bounded)
    def _():
        for qb in range(s // FOX_TQ):
            r0 = qb * FOX_TQ
            rows = slice(r0, r0 + FOX_TQ)
            qa, ka, v1 = prepare(rows)
            diag = jnp.where(causal, _dot_nt(qa, ka), NEG_BIG)
            ol = _dot(jnp.exp2(diag).astype(BF16), v1)
            if r0:
                past = _dot_nt(qa, ka_scr[:r0, :])
                ol = ol + _dot(jnp.exp2(past).astype(BF16), v1_scr[:r0, :])
            finish(rows, ol)

    @pl.when(jnp.logical_not(bounded))
    def _():
        for qb in range(s // FOX_TQ):
            r0 = qb * FOX_TQ
            rows = slice(r0, r0 + FOX_TQ)
            qa, ka, v1 = prepare(rows)
            qn = qa[:, :HEAD_DIM]
            diag = _dot_nt(qn, ka[:, :HEAD_DIM]) - crow_ref[:, rows]
            diag = jnp.where(causal, diag, NEG_BIG)
            m = jnp.max(diag, axis=-1, keepdims=True)
            if r0:
                past = _dot_nt(qn, ka_scr[:r0, :HEAD_DIM]) - crow_ref[:, :r0]
                m = jnp.maximum(m, jnp.max(past, axis=-1, keepdims=True))
                ol = _dot(jnp.exp2(past - m).astype(BF16), v1_scr[:r0, :])
                ol = ol + _dot(jnp.exp2(diag - m).astype(BF16), v1)
            else:
                ol = _dot(jnp.exp2(diag - m).astype(BF16), v1)
            finish(rows, ol)


def _fox(main, aq, ak, crow4, gq, gk, ong, batch, seq):
    t = main.shape[0]
    cb = lambda off: off // HEAD_DIM
    return pl.pallas_call(
        _fox_kernel,
        grid=(batch, FOX_HEADS),
        in_specs=[
            pl.BlockSpec((seq, HEAD_DIM), lambda b, h: (b, cb(OFF_FQ) + h)),
            pl.BlockSpec((seq, HEAD_DIM), lambda b, h: (b, cb(OFF_FK) + h)),
            pl.BlockSpec((seq, HEAD_DIM), lambda b, h: (b, cb(OFF_FV) + h)),
            pl.BlockSpec((seq, HEAD_DIM), lambda b, h: (b, cb(OFF_FG) + h)),
            pl.BlockSpec((seq, LANES), lambda b, h: (b, 0)),
            pl.BlockSpec((seq, LANES), lambda b, h: (b, 0)),
            pl.BlockSpec((None, None, 1, seq), lambda b, h: (b, h, 0, 0)),
            pl.BlockSpec((1, HEAD_DIM), lambda b, h: (0, 0)),
            pl.BlockSpec((1, HEAD_DIM), lambda b, h: (0, 0)),
            pl.BlockSpec((None, 1, HEAD_DIM), lambda b, h: (h, 0, 0)),
        ],
        out_specs=pl.BlockSpec((seq, HEAD_DIM), lambda b, h: (b, h)),
        out_shape=jax.ShapeDtypeStruct((t, FOX_HEADS * HEAD_DIM), BF16),
        scratch_shapes=[
            pltpu.VMEM((seq, HEAD_DIM + LANES), BF16),
            pltpu.VMEM((seq, 2 * HEAD_DIM), BF16),
        ],
        compiler_params=_params(("parallel", "arbitrary")),
        name="fox",
    )(main, main, main, main, aq, ak, crow4, gq, gk, ong)


def _memkv_kernel(mem_ref, g_ref, w_ref, gk_ref, mk_ref, mv_ref):
    mn = _rms(mem_ref[...], g_ref[...]).astype(BF16)
    kv = _dot(mn, w_ref[...])
    dk = MEM_HEADS * HEAD_DIM
    ones = jnp.ones((kv.shape[0], HEAD_DIM), BF16)
    for hh in range(MEM_HEADS):
        hl = slice(hh * HEAD_DIM, (hh + 1) * HEAD_DIM)
        mk_ref[:, hl] = _rms(kv[:, hl], gk_ref[...]).astype(BF16)
        mv_ref[:, 2 * hh * HEAD_DIM:(2 * hh + 1) * HEAD_DIM] = kv[:, dk + hh * HEAD_DIM:dk + (hh + 1) * HEAD_DIM].astype(BF16)
        mv_ref[:, (2 * hh + 1) * HEAD_DIM:(2 * hh + 2) * HEAD_DIM] = ones


def _mem_kv(mem2, g, w, gk, batch, n_mem):
    d = mem2.shape[1]
    dk = MEM_HEADS * HEAD_DIM
    return pl.pallas_call(
        _memkv_kernel,
        grid=(batch,),
        in_specs=[
            pl.BlockSpec((n_mem, d), lambda b: (b, 0)),
            pl.BlockSpec((1, d), lambda b: (0, 0)),
            pl.BlockSpec((d, 2 * dk), lambda b: (0, 0)),
            pl.BlockSpec((1, HEAD_DIM), lambda b: (0, 0)),
        ],
        out_specs=[
            pl.BlockSpec((n_mem, dk), lambda b: (b, 0)),
            pl.BlockSpec((n_mem, 2 * dk), lambda b: (b, 0)),
        ],
        out_shape=[
            jax.ShapeDtypeStruct((batch * n_mem, dk), BF16),
            jax.ShapeDtypeStruct((batch * n_mem, 2 * dk), BF16),
        ],
        compiler_params=_params(("parallel",)),
        name="mem_kv",
    )(mem2, g, w, gk)


def _memattn_kernel(q_ref, mg_ref, mk_ref, mv_ref, gq_ref, ong_ref, o_ref):
    s = q_ref.shape[0]
    scale = HEAD_DIM ** -0.5 * LOG2E
    for qb in range(s // MEM_TQ):
        rows = slice(qb * MEM_TQ, (qb + 1) * MEM_TQ)
        qn = (_rms(q_ref[rows, :].astype(F32), gq_ref[...]) * scale).astype(BF16)
        logits = _dot_nt(qn, mk_ref[...])
        m = jnp.max(logits, axis=-1, keepdims=True)
        ol = _dot(jnp.exp2(logits - m).astype(BF16), mv_ref[...])
        o = ol[:, :HEAD_DIM] / ol[:, HEAD_DIM:]
        on = _rms(o, ong_ref[...])
        o_ref[rows, :] = (on * _sigmoid(mg_ref[rows, :].astype(F32))).astype(BF16)


def _mem_attn(main, mkn, mv, gq, ong, batch, seq, n_mem):
    t = main.shape[0]
    cb = lambda off: off // HEAD_DIM
    return pl.pallas_call(
        _memattn_kernel,
        grid=(batch, MEM_HEADS),
        in_specs=[
            pl.BlockSpec((seq, HEAD_DIM), lambda b, h: (b, cb(OFF_MQ) + h)),
            pl.BlockSpec((seq, HEAD_DIM), lambda b, h: (b, cb(OFF_MG) + h)),
            pl.BlockSpec((n_mem, HEAD_DIM), lambda b, h: (b, h)),
            pl.BlockSpec((n_mem, 2 * HEAD_DIM), lambda b, h: (b, h)),
            pl.BlockSpec((1, HEAD_DIM), lambda b, h: (0, 0)),
            pl.BlockSpec((None, 1, HEAD_DIM), lambda b, h: (h, 0, 0)),
        ],
        out_specs=pl.BlockSpec((seq, HEAD_DIM), lambda b, h: (b, h)),
        out_shape=jax.ShapeDtypeStruct((t, MEM_HEADS * HEAD_DIM), BF16),
        compiler_params=_params(("parallel", "arbitrary")),
        name="mem_attn",
    )(main, main, mkn, mv, gq, ong)


def _outproj_kernel(og_ref, of_ref, om_ref, x_ref, wo_ref, gm_ref, h_ref, hn_ref):
    n_g = og_ref.shape[1]
    n_f = of_ref.shape[1]
    acc = _dot(og_ref[...], wo_ref[:n_g, :])
    acc = acc + _dot(of_ref[...], wo_ref[n_g:n_g + n_f, :])
    acc = acc + _dot(om_ref[...], wo_ref[n_g + n_f:, :])
    h = x_ref[...] + acc
    h_ref[...] = h
    hn_ref[...] = _rms(h, gm_ref[...]).astype(BF16)


def _out_proj(og, of, om, x2, wo, gm, tm=512):
    t, d = x2.shape
    row = lambda i: (i, 0)
    fixed = lambda i: (0, 0)
    return pl.pallas_call(
        _outproj_kernel,
        grid=(t // tm,),
        in_specs=[
            pl.BlockSpec((tm, og.shape[1]), row),
            pl.BlockSpec((tm, of.shape[1]), row),
            pl.BlockSpec((tm, om.shape[1]), row),
            pl.BlockSpec((tm, d), row),
            pl.BlockSpec(wo.shape, fixed),
            pl.BlockSpec((1, d), fixed),
        ],
        out_specs=[pl.BlockSpec((tm, d), row), pl.BlockSpec((tm, d), row)],
        out_shape=[jax.ShapeDtypeStruct((t, d), F32), jax.ShapeDtypeStruct((t, d), BF16)],
        compiler_params=_params(("parallel",)),
        name="out_proj",
    )(og, of, om, x2, wo, gm)


def _mlp_kernel(hn_ref, h_ref, wu_ref, wd_ref, o_ref):
    @pl.when(pl.program_id(1) == 0)
    def _():
        o_ref[...] = h_ref[...]

    u = jnp.maximum(_dot(hn_ref[...], wu_ref[...]), 0.0)
    o_ref[...] += _dot((u * u).astype(BF16), wd_ref[...])


def _mlp(hn, h, wu, wd, tm=512, tf=1024):
    t, d = h.shape
    dff = wu.shape[1]
    return pl.pallas_call(
        _mlp_kernel,
        grid=(t // tm, dff // tf),
        in_specs=[
            pl.BlockSpec((tm, d), lambda i, f: (i, 0)),
            pl.BlockSpec((tm, d), lambda i, f: (i, 0)),
            pl.BlockSpec((d, tf), lambda i, f: (0, f)),
            pl.BlockSpec((tf, d), lambda i, f: (f, 0)),
        ],
        out_specs=pl.BlockSpec((tm, d), lambda i, f: (i, 0)),
        out_shape=jax.ShapeDtypeStruct((t, d), F32),
        compiler_params=_params(("parallel", "arbitrary")),
        name="mlp",
    )(hn, h, wu, wd)


def _regroup_w_in(w_in):
    n_gla = OFF_FQ
    n_fox = OFF_MQ - OFF_FQ
    a0 = n_gla
    f0 = a0 + GLA_RANK
    ff0 = f0 + n_fox
    m0 = ff0 + FOX_HEADS
    wb = w_in.astype(BF16)
    main = jnp.concatenate([wb[:, :n_gla], wb[:, f0:ff0], wb[:, m0:]], axis=1)
    pad = jnp.zeros((w_in.shape[0], LANES - GLA_RANK - FOX_HEADS), BF16)
    small = jnp.concatenate([wb[:, a0:f0], wb[:, ff0:m0], pad], axis=1)
    return main, small


def kernel(x, mem, attn_norm_g, w_in, gla_a_w2, gla_a_b, fox_f_b, fox_q_norm_g, fox_k_norm_g,
           mem_norm_g, w_mem_kv, mem_q_norm_g, mem_k_norm_g, out_norm_g, w_out, mlp_norm_g,
           w_up, w_down):
    batch, seq, d = x.shape
    n_mem = mem.shape[1]
    t = batch * seq
    x2 = x.reshape(t, d)
    row = lambda a: a.reshape(1, -1).astype(F32)

    w_main, w_small = _regroup_w_in(w_in)
    main, small = _in_proj(x2, row(attn_norm_g), w_main, w_small)

    n_gla = GLA_HEADS * HEAD_DIM
    n_fox = FOX_HEADS * HEAD_DIM
    w2p = jnp.zeros((LANES, GLA_HEADS * GLA_DK), F32).at[:GLA_RANK].set(gla_a_w2).astype(BF16)
    og = _gla(main, small, w2p, row(gla_a_b), row(out_norm_g[:n_gla]), batch, seq)

    fb = jnp.zeros((1, LANES), F32).at[0, SMALL_FF:SMALL_FF + FOX_HEADS].set(fox_f_b)
    crow, aq, ak = _fox_gate(small, fb, row(fox_q_norm_g), row(fox_k_norm_g), batch, seq)
    ong_fox = out_norm_g[n_gla:n_gla + n_fox].reshape(FOX_HEADS, 1, HEAD_DIM)
    of = _fox(main, aq, ak, crow.reshape(batch, SUBLANES, 1, seq), row(fox_q_norm_g),
              row(fox_k_norm_g), ong_fox, batch, seq)

    mkn, mv = _mem_kv(mem.reshape(batch * n_mem, d), row(mem_norm_g), w_mem_kv.astype(BF16),
                      row(mem_k_norm_g), batch, n_mem)
    ong_mem = out_norm_g[n_gla + n_fox:].reshape(MEM_HEADS, 1, HEAD_DIM)
    om = _mem_attn(main, mkn, mv, row(mem_q_norm_g), ong_mem, batch, seq, n_mem)

    h, hn = _out_proj(og, of, om, x2, w_out.astype(BF16), row(mlp_norm_g))
    y = _mlp(hn, h, w_up.astype(BF16), w_down.astype(BF16))
    return y.reshape(batch, seq, d)
```

```python
import functools

import jax
import jax.numpy as jnp
import numpy as np
from jax import lax
from jax.experimental import pallas as pl
from jax.experimental.pallas import tpu as pltpu

F32 = jnp.float32
BF16 = jnp.bfloat16

HEAD_DIM = 128
GLA_HEADS = 8
GLA_DK = 64
GLA_RANK = 16
GLA_TAU = 16.0
FOX_HEADS = 4
MEM_HEADS = 4
EPS = 1e-6

LANES = 128
SUBLANES = 8
LOG2E = 1.4426950408889634
GLA_CHUNK = 64
GLA_STEP = 512
GLA_SAFE_EXP = 60.0
FOX_TQ = 256
FOX_AUG = 8
FOX_MAX_SPAN = 100.0
MEM_TQ = 512
NEG_BIG = -1e30
VMEM_LIMIT = 50 * 1024 * 1024

OFF_GQ, OFF_GK, OFF_GV, OFF_GG = 0, 512, 1024, 2048
OFF_FQ, OFF_FK, OFF_FV, OFF_FG = 3072, 3584, 4096, 4608
OFF_MQ, OFF_MG = 5120, 5632
N_MAIN = 6144
SMALL_FF = GLA_RANK
FF_ROW0 = SMALL_FF
assert FF_ROW0 % SUBLANES == 0 and FOX_HEADS <= SUBLANES


def _dot(a, b):
    return jnp.dot(a, b, preferred_element_type=F32)


def _dot_nt(a, b):
    return lax.dot_general(a, b, (((1,), (1,)), ((), ())), preferred_element_type=F32)


def _dot_tn(a, b):
    return lax.dot_general(a, b, (((0,), (0,)), ((), ())), preferred_element_type=F32)


def _rms(x, g):
    return x * lax.rsqrt(jnp.mean(x * x, axis=-1, keepdims=True) + EPS) * g


def _log_sigmoid(z):
    return jnp.minimum(z, 0.0) - jnp.log(1.0 + jnp.exp(-jnp.abs(z)))


def _sigmoid(z):
    return 1.0 / (1.0 + jnp.exp(-z))


def _split3(a):
    hi = a.astype(BF16)
    r = a - hi.astype(F32)
    mid = r.astype(BF16)
    lo = (r - mid.astype(F32)).astype(BF16)
    return hi, mid, lo


def _params(sem):
    return pltpu.CompilerParams(dimension_semantics=sem, vmem_limit_bytes=VMEM_LIMIT)


def _inproj_kernel(x_ref, g_ref, wm_ref, ws_ref, main_ref, small_ref, xn_even, xn_odd):
    i = pl.program_id(0)
    j = pl.program_id(1)
    n_col = pl.num_programs(1)
    rs = x_ref.shape[0] // n_col
    odd = i % 2 == 1

    def normalise_slice(dst):
        rows = pl.ds(pl.multiple_of(j * rs, rs), rs)
        dst[rows, :] = _rms(x_ref[rows, :], g_ref[...]).astype(BF16)

    def step(src, dst):
        @pl.when(j == 0)
        def _():
            small_ref[...] = _dot(src[...], ws_ref[...])

        normalise_slice(dst)
        main_ref[...] = _dot(src[...], wm_ref[...]).astype(BF16)

    @pl.when(i == 0)
    def _():
        normalise_slice(xn_even)

    @pl.when(odd)
    def _():
        step(xn_even, xn_odd)

    @pl.when((i > 0) & jnp.logical_not(odd))
    def _():
        step(xn_odd, xn_even)


def _in_proj(x2, g, w_main, w_small, tm=1024, tn=1536):
    t, d = x2.shape
    n_row = t // tm
    prev = lambda i: jnp.maximum(i - 1, 0)
    return pl.pallas_call(
        _inproj_kernel,
        grid=(n_row + 1, N_MAIN // tn),
        in_specs=[
            pl.BlockSpec((tm, d), lambda i, j: (jnp.minimum(i, n_row - 1), 0)),
            pl.BlockSpec((1, d), lambda i, j: (0, 0)),
            pl.BlockSpec((d, tn), lambda i, j: (0, j)),
            pl.BlockSpec((d, LANES), lambda i, j: (0, 0)),
        ],
        out_specs=[
            pl.BlockSpec((tm, tn), lambda i, j: (prev(i), jnp.where(i > 0, j, 0))),
            pl.BlockSpec((tm, LANES), lambda i, j: (prev(i), 0)),
        ],
        out_shape=[
            jax.ShapeDtypeStruct((t, N_MAIN), BF16),
            jax.ShapeDtypeStruct((t, LANES), F32),
        ],
        scratch_shapes=[pltpu.VMEM((tm, d), BF16), pltpu.VMEM((tm, d), BF16)],
        compiler_params=_params(("arbitrary", "arbitrary")),
        name="in_proj",
    )(x2, g, w_main, w_small)


def _gla_kernel(q_ref, k_ref, v_ref, gg_ref, small_ref, small_next_ref, w2_ref, ab_ref, ong_ref,
                o_ref, st_ref, bc_scr, safe_ref, qs_scr, bs_scr, os_scr):
    C = GLA_CHUNK
    n_pairs = GLA_HEADS // 2
    step = pl.program_id(0) * pl.num_programs(1) + pl.program_id(1)
    slot = step % 2

    @pl.when(pl.program_id(1) == 0)
    def _():
        st_ref[...] = jnp.zeros_like(st_ref)

    def decay_sums(src_ref, dst):
        z = _dot(src_ref[...].astype(BF16), w2_ref[...]) + ab_ref[...]
        la = _log_sigmoid(z) * (1.0 / GLA_TAU)
        ri = lax.broadcasted_iota(jnp.int32, (GLA_STEP, GLA_STEP), 0)
        ci = lax.broadcasted_iota(jnp.int32, (GLA_STEP, GLA_STEP), 1)
        same_chunk = (ri ^ ci) < C
        tri = jnp.where(same_chunk, jnp.where(ci <= ri, 1.0, 0.0), 0.0).astype(BF16)
        hi, mid, lo = _split3(la)
        bc_scr[dst] = _dot(tri, hi) + _dot(tri, mid) + _dot(tri, lo)
        safe_ref[dst] = (jnp.min(la) * C >= -GLA_SAFE_EXP).astype(jnp.int32)

    @pl.when(step == 0)
    def _():
        decay_sums(small_ref, 0)

    lane = lax.broadcasted_iota(jnp.int32, (C, LANES), 1)
    low_half = lane < GLA_DK
    st_r = lax.broadcasted_iota(jnp.int32, (2 * HEAD_DIM, LANES), 0)
    st_c = lax.broadcasted_iota(jnp.int32, (2 * HEAD_DIM, LANES), 1)
    st_mask = (st_r < HEAD_DIM) == (st_c < GLA_DK)
    a_r = lax.broadcasted_iota(jnp.int32, (C, 2 * C), 0)
    a_c = lax.broadcasted_iota(jnp.int32, (C, 2 * C), 1)
    causal = a_r >= (a_c & (C - 1))
    zeros_v = jnp.zeros((C, HEAD_DIM), BF16)
    jrow = lax.broadcasted_iota(jnp.int32, (C, LANES), 0)

    def unit(ch, p, fast):
        rows = slice(ch * C, (ch + 1) * C)
        kl = slice(p * LANES, (p + 1) * LANES)
        vl = slice(p * 2 * HEAD_DIM, (p + 1) * 2 * HEAD_DIM)
        b = bc_scr[slot, rows, kl]
        qs = q_ref[rows, kl].astype(F32) * (GLA_DK ** -0.5)
        kf = k_ref[rows, kl].astype(F32)
        v2 = v_ref[rows, vl]
        qe16 = (qs * jnp.exp(b)).astype(BF16)
        b_last = b[C - 1:C, :]
        st = st_ref[p]
        o = _dot_nt(qe16, st.astype(BF16))
        if fast:
            ke = kf * jnp.exp(-b)
            kbd = jnp.concatenate(
                [jnp.where(low_half, ke, 0.0), jnp.where(low_half, 0.0, ke)], axis=0).astype(BF16)
            a = jnp.where(causal, _dot_nt(qe16, kbd), 0.0).astype(BF16)
            vbd = jnp.concatenate(
                [jnp.concatenate([v2[:, :HEAD_DIM], zeros_v], axis=1),
                 jnp.concatenate([zeros_v, v2[:, HEAD_DIM:]], axis=1)], axis=0)
            o = o + _dot(a, vbd)
            st_ref[p] = jnp.exp(b_last) * (st + _dot_tn(vbd, kbd))
        else:
            kd16 = (kf * jnp.exp(b_last - b)).astype(BF16)
            upd = _dot_tn(v2, kd16)
            st_ref[p] = jnp.exp(b_last) * st + jnp.where(st_mask, upd, 0.0)
            qs_scr[...] = qs
            bs_scr[...] = b
            v2f = v2.astype(F32)

            def row(i, carry):
                qi = qs_scr[pl.ds(i, 1), :]
                bi = bs_scr[pl.ds(i, 1), :]
                w = qi * kf * jnp.exp(jnp.minimum(bi - b, 0.0))
                w = jnp.where(jrow <= i, w, 0.0)
                p0 = jnp.sum(jnp.where(low_half, w, 0.0), axis=1, keepdims=True)
                p1 = jnp.sum(jnp.where(low_half, 0.0, w), axis=1, keepdims=True)
                pv = jnp.concatenate([p0 * v2f[:, :HEAD_DIM], p1 * v2f[:, HEAD_DIM:]], axis=1)
                os_scr[pl.ds(i, 1), :] = jnp.sum(pv, axis=0, keepdims=True)
                return carry

            lax.fori_loop(0, C, row, 0)
            o = o + os_scr[...]
        for hh in range(2):
            hl = slice((2 * p + hh) * HEAD_DIM, (2 * p + hh + 1) * HEAD_DIM)
            on = _rms(o[:, hh * HEAD_DIM:(hh + 1) * HEAD_DIM], ong_ref[:, hl])
            g = gg_ref[rows, hl].astype(F32)
            o_ref[rows, hl] = (on * (g * _sigmoid(g))).astype(BF16)

    def run(fast):
        decay_sums(small_next_ref, 1 - slot)
        for ch in range(GLA_STEP // C):
            for p in range(n_pairs):
                unit(ch, p, fast)

    safe = safe_ref[slot]

    @pl.when(safe == 1)
    def _():
        run(True)

    @pl.when(safe == 0)
    def _():
        run(False)


def _gla(main, small, w2p, ab, ong, batch, seq):
    t = main.shape[0]
    nc = seq // GLA_STEP
    rb = lambda b, c: b * nc + c
    rb_next = lambda b, c: jnp.minimum(b * nc + c + 1, batch * nc - 1)
    dq = GLA_HEADS * GLA_DK
    dv = GLA_HEADS * HEAD_DIM
    return pl.pallas_call(
        _gla_kernel,
        grid=(batch, nc),
        in_specs=[
            pl.BlockSpec((GLA_STEP, dq), lambda b, c: (rb(b, c), OFF_GQ // dq)),
            pl.BlockSpec((GLA_STEP, dq), lambda b, c: (rb(b, c), OFF_GK // dq)),
            pl.BlockSpec((GLA_STEP, dv), lambda b, c: (rb(b, c), OFF_GV // dv)),
            pl.BlockSpec((GLA_STEP, dv), lambda b, c: (rb(b, c), OFF_GG // dv)),
            pl.BlockSpec((GLA_STEP, LANES), lambda b, c: (rb(b, c), 0)),
            pl.BlockSpec((GLA_STEP, LANES), lambda b, c: (rb_next(b, c), 0)),
            pl.BlockSpec((LANES, dq), lambda b, c: (0, 0)),
            pl.BlockSpec((1, dq), lambda b, c: (0, 0)),
            pl.BlockSpec((1, dv), lambda b, c: (0, 0)),
        ],
        out_specs=pl.BlockSpec((GLA_STEP, dv), lambda b, c: (rb(b, c), 0)),
        out_shape=jax.ShapeDtypeStruct((t, dv), BF16),
        scratch_shapes=[
            pltpu.VMEM((GLA_HEADS // 2, 2 * HEAD_DIM, LANES), F32),
            pltpu.VMEM((2, GLA_STEP, dq), F32),
            pltpu.SMEM((2,), jnp.int32),
            pltpu.VMEM((GLA_CHUNK, LANES), F32),
            pltpu.VMEM((GLA_CHUNK, LANES), F32),
            pltpu.VMEM((GLA_CHUNK, 2 * HEAD_DIM), F32),
        ],
        compiler_params=_params(("arbitrary", "arbitrary")),
        name="gla",
    )(main, main, main, main, small, small, w2p, ab, ong)


def _foxgate_kernel(small_ref, fb_ref, gq_ref, gk_ref, p_ref, rows_ref,
                    crow_ref, aq_ref, ak_ref, c_scr):
    s = small_ref.shape[0]
    blk = 256
    ri = lax.broadcasted_iota(jnp.int32, (blk, blk), 0)
    ci = lax.broadcasted_iota(jnp.int32, (blk, blk), 1)
    tri = jnp.where(ci <= ri, 1.0, 0.0).astype(BF16)
    carry = jnp.zeros((1, LANES), F32)
    qk_bound = _qk_bound(gq_ref[...], gk_ref[...])
    b_hi = qk_bound.astype(BF16).astype(F32)
    b_lo = (qk_bound - b_hi).astype(BF16).astype(F32)
    q_const = rows_ref[0:1, :]
    k_const = rows_ref[1:2, :] + b_hi * rows_ref[2:3, :] + b_lo * rows_ref[3:4, :]
    for i in range(s // blk):
        rows = slice(i * blk, (i + 1) * blk)
        lf = _log_sigmoid(small_ref[rows, :] + fb_ref[...]) * LOG2E
        hi, mid, lo = _split3(lf)
        c = _dot(tri, hi) + _dot(tri, mid) + _dot(tri, lo) + carry
        c_scr[rows, :] = c
        carry = c[blk - 1:blk, :]
        aug = _dot(jnp.concatenate(_split3(c), axis=1), p_ref[...])
        aq_ref[rows, :] = (aug[:, :LANES] + q_const).astype(BF16)
        ak_ref[rows, :] = (aug[:, LANES:] + k_const).astype(BF16)
    crow_ref[...] = c_scr[...].T[FF_ROW0:FF_ROW0 + SUBLANES, :]


def _qk_bound(gq, gk):
    gmax = jnp.max(jnp.abs(gq), axis=-1, keepdims=True) * jnp.max(jnp.abs(gk), axis=-1, keepdims=True)
    return gmax * (1.01 * LOG2E * HEAD_DIM ** 0.5)


def _fox_aug_constants():
    p = np.zeros((3 * LANES, 2 * LANES), np.float32)
    rows = np.zeros((SUBLANES, LANES), np.float32)
    for h in range(FOX_HEADS):
        base = FOX_AUG * h
        for j in range(3):
            p[j * LANES + SMALL_FF + h, base + j] = 1.0
            p[j * LANES + SMALL_FF + h, LANES + base + 3 + j] = -1.0
            rows[1, base + j] = 1.0
        rows[0, base + 3:base + 8] = 1.0
        rows[2, base + 6] = -1.0
        rows[3, base + 7] = -1.0
    return jnp.asarray(p, BF16), jnp.asarray(rows)


def _fox_gate(small, fb, gq, gk, batch, seq):
    t = small.shape[0]
    p, const_rows = _fox_aug_constants()
    fixed = lambda b: (0, 0)
    return pl.pallas_call(
        _foxgate_kernel,
        grid=(batch,),
        in_specs=[
            pl.BlockSpec((seq, LANES), lambda b: (b, 0)),
            pl.BlockSpec((1, LANES), fixed),
            pl.BlockSpec((1, HEAD_DIM), fixed),
            pl.BlockSpec((1, HEAD_DIM), fixed),
            pl.BlockSpec(p.shape, fixed),
            pl.BlockSpec(const_rows.shape, fixed),
        ],
        out_specs=[
            pl.BlockSpec((None, SUBLANES, seq), lambda b: (b, 0, 0)),
            pl.BlockSpec((seq, LANES), lambda b: (b, 0)),
            pl.BlockSpec((seq, LANES), lambda b: (b, 0)),
        ],
        out_shape=[
            jax.ShapeDtypeStruct((batch, SUBLANES, seq), F32),
            jax.ShapeDtypeStruct((t, LANES), BF16),
            jax.ShapeDtypeStruct((t, LANES), BF16),
        ],
        scratch_shapes=[pltpu.VMEM((seq, LANES), F32)],
        compiler_params=_params(("parallel",)),
        name="fox_gate",
    )(small, fb, gq, gk, p, const_rows)


def _fox_kernel(q_ref, k_ref, v_ref, fg_ref, aq_ref, ak_ref, crow_ref, gq_ref, gk_ref, ong_ref,
                o_ref, ka_scr, v1_scr):
    s = q_ref.shape[0]
    h = pl.program_id(1)
    scale = HEAD_DIM ** -0.5 * LOG2E
    lane = lax.broadcasted_iota(jnp.int32, (FOX_TQ, LANES), 1)
    own = (lane >= FOX_AUG * h) & (lane < FOX_AUG * (h + 1))
    qi = lax.broadcasted_iota(jnp.int32, (FOX_TQ, FOX_TQ), 0)
    ki = lax.broadcasted_iota(jnp.int32, (FOX_TQ, FOX_TQ), 1)
    causal = qi >= ki
    ones = jnp.ones((FOX_TQ, HEAD_DIM), BF16)

    def prepare(rows):
        qn = (_rms(q_ref[rows, :].astype(F32), gq_ref[...]) * scale).astype(BF16)
        kn = _rms(k_ref[rows, :].astype(F32), gk_ref[...]).astype(BF16)
        qa = jnp.concatenate(
            [qn, jnp.where(own, aq_ref[rows, :].astype(F32), 0.0).astype(BF16)], axis=1)
        ka = jnp.concatenate([kn, ak_ref[rows, :]], axis=1)
        v1 = jnp.concatenate([v_ref[rows, :], ones], axis=1)
        ka_scr[rows, :] = ka
        v1_scr[rows, :] = v1
        return qa, ka, v1

    def finish(rows, ol):
        o = ol[:, :HEAD_DIM] / ol[:, HEAD_DIM:]
        on = _rms(o, ong_ref[...])
        o_ref[rows, :] = (on * _sigmoid(fg_ref[rows, :].astype(F32))).astype(BF16)

    bounded = 2.0 * _qk_bound(gq_ref[...], gk_ref[...])[0, 0] <= FOX_MAX_SPAN

    @pl.when(bounded)
    def _():
        for qb in range(s // FOX_TQ):
            r0 = qb * FOX_TQ
            rows = slice(r0, r0 + FOX_TQ)
            qa, ka, v1 = prepare(rows)
            diag = jnp.where(causal, _dot_nt(qa, ka), NEG_BIG)
            ol = _dot(jnp.exp2(diag).astype(BF16), v1)
            if r0:
                past = _dot_nt(qa, ka_scr[:r0, :])
                ol = ol + _dot(jnp.exp2(past).astype(BF16), v1_scr[:r0, :])
            finish(rows, ol)

    @pl.when(jnp.logical_not(bounded))
    def _():
        for qb in range(s // FOX_TQ):
            r0 = qb * FOX_TQ
            rows = slice(r0, r0 + FOX_TQ)
            qa, ka, v1 = prepare(rows)
            qn = qa[:, :HEAD_DIM]
            diag = _dot_nt(qn, ka[:, :HEAD_DIM]) - crow_ref[:, rows]
            diag = jnp.where(causal, diag, NEG_BIG)
            m = jnp.max(diag, axis=-1, keepdims=True)
            if r0:
                past = _dot_nt(qn, ka_scr[:r0, :HEAD_DIM]) - crow_ref[:, :r0]
                m = jnp.maximum(m, jnp.max(past, axis=-1, keepdims=True))
                ol = _dot(jnp.exp2(past - m).astype(BF16), v1_scr[:r0, :])
                ol = ol + _dot(jnp.exp2(diag - m).astype(BF16), v1)
            else:
                ol = _dot(jnp.exp2(diag - m).astype(BF16), v1)
            finish(rows, ol)


def _fox(main, aq, ak, crow4, gq, gk, ong, batch, seq):
    t = main.shape[0]
    cb = lambda off: off // HEAD_DIM
    return pl.pallas_call(
        _fox_kernel,
        grid=(batch, FOX_HEADS),
        in_specs=[
            pl.BlockSpec((seq, HEAD_DIM), lambda b, h: (b, cb(OFF_FQ) + h)),
            pl.BlockSpec((seq, HEAD_DIM), lambda b, h: (b, cb(OFF_FK) + h)),
            pl.BlockSpec((seq, HEAD_DIM), lambda b, h: (b, cb(OFF_FV) + h)),
            pl.BlockSpec((seq, HEAD_DIM), lambda b, h: (b, cb(OFF_FG) + h)),
            pl.BlockSpec((seq, LANES), lambda b, h: (b, 0)),
            pl.BlockSpec((seq, LANES), lambda b, h: (b, 0)),
            pl.BlockSpec((None, None, 1, seq), lambda b, h: (b, h, 0, 0)),
            pl.BlockSpec((1, HEAD_DIM), lambda b, h: (0, 0)),
            pl.BlockSpec((1, HEAD_DIM), lambda b, h: (0, 0)),
            pl.BlockSpec((None, 1, HEAD_DIM), lambda b, h: (h, 0, 0)),
        ],
        out_specs=pl.BlockSpec((seq, HEAD_DIM), lambda b, h: (b, h)),
        out_shape=jax.ShapeDtypeStruct((t, FOX_HEADS * HEAD_DIM), BF16),
        scratch_shapes=[
            pltpu.VMEM((seq, HEAD_DIM + LANES), BF16),
            pltpu.VMEM((seq, 2 * HEAD_DIM), BF16),
        ],
        compiler_params=_params(("parallel", "arbitrary")),
        name="fox",
    )(main, main, main, main, aq, ak, crow4, gq, gk, ong)


def _memkv_kernel(mem_ref, g_ref, w_ref, gq_ref, gk_ref, mk_ref, mv_ref):
    mn = _rms(mem_ref[...], g_ref[...]).astype(BF16)
    kv = _dot(mn, w_ref[...])
    n_mem = kv.shape[0]
    dk = MEM_HEADS * HEAD_DIM
    ones = jnp.ones((n_mem, HEAD_DIM), BF16)
    qk_bound = _qk_bound(gq_ref[...], gk_ref[...])
    b_hi = qk_bound.astype(BF16).astype(F32)
    b_lo = (qk_bound - b_hi).astype(BF16).astype(F32)
    lane = lax.broadcasted_iota(jnp.int32, (n_mem, LANES), 1)
    shift = jnp.where(lane == 0, -b_hi, jnp.where(lane == 1, -b_lo, 0.0)).astype(BF16)
    for hh in range(MEM_HEADS):
        hl = slice(hh * HEAD_DIM, (hh + 1) * HEAD_DIM)
        mk_ref[:, 2 * hh * HEAD_DIM:(2 * hh + 1) * HEAD_DIM] = _rms(kv[:, hl], gk_ref[...]).astype(BF16)
        mk_ref[:, (2 * hh + 1) * HEAD_DIM:(2 * hh + 2) * HEAD_DIM] = shift
        mv_ref[:, 2 * hh * HEAD_DIM:(2 * hh + 1) * HEAD_DIM] = kv[:, dk + hh * HEAD_DIM:dk + (hh + 1) * HEAD_DIM].astype(BF16)
        mv_ref[:, (2 * hh + 1) * HEAD_DIM:(2 * hh + 2) * HEAD_DIM] = ones


def _mem_kv(mem2, g, w, gq, gk, batch, n_mem):
    d = mem2.shape[1]
    dk = MEM_HEADS * HEAD_DIM
    return pl.pallas_call(
        _memkv_kernel,
        grid=(batch,),
        in_specs=[
            pl.BlockSpec((n_mem, d), lambda b: (b, 0)),
            pl.BlockSpec((1, d), lambda b: (0, 0)),
            pl.BlockSpec((d, 2 * dk), lambda b: (0, 0)),
            pl.BlockSpec((1, HEAD_DIM), lambda b: (0, 0)),
            pl.BlockSpec((1, HEAD_DIM), lambda b: (0, 0)),
        ],
        out_specs=[
            pl.BlockSpec((n_mem, 2 * dk), lambda b: (b, 0)),
            pl.BlockSpec((n_mem, 2 * dk), lambda b: (b, 0)),
        ],
        out_shape=[
            jax.ShapeDtypeStruct((batch * n_mem, 2 * dk), BF16),
            jax.ShapeDtypeStruct((batch * n_mem, 2 * dk), BF16),
        ],
        compiler_params=_params(("parallel",)),
        name="mem_kv",
    )(mem2, g, w, gq, gk)


def _memattn_kernel(q_ref, mg_ref, mk_ref, mv_ref, gq_ref, gk_ref, ong_ref, o_ref):
    s = q_ref.shape[0]
    scale = HEAD_DIM ** -0.5 * LOG2E
    lane = lax.broadcasted_iota(jnp.int32, (MEM_TQ, LANES), 1)
    q_extra = jnp.where(lane < 2, 1.0, 0.0).astype(BF16)

    def q_block(rows):
        return (_rms(q_ref[rows, :].astype(F32), gq_ref[...]) * scale).astype(BF16)

    def finish(rows, ol):
        o = ol[:, :HEAD_DIM] / ol[:, HEAD_DIM:]
        on = _rms(o, ong_ref[...])
        o_ref[rows, :] = (on * _sigmoid(mg_ref[rows, :].astype(F32))).astype(BF16)

    bounded = 2.0 * _qk_bound(gq_ref[...], gk_ref[...])[0, 0] <= FOX_MAX_SPAN

    @pl.when(bounded)
    def _():
        for qb in range(s // MEM_TQ):
            rows = slice(qb * MEM_TQ, (qb + 1) * MEM_TQ)
            qa = jnp.concatenate([q_block(rows), q_extra], axis=1)
            finish(rows, _dot(jnp.exp2(_dot_nt(qa, mk_ref[...])).astype(BF16), mv_ref[...]))

    @pl.when(jnp.logical_not(bounded))
    def _():
        for qb in range(s // MEM_TQ):
            rows = slice(qb * MEM_TQ, (qb + 1) * MEM_TQ)
            logits = _dot_nt(q_block(rows), mk_ref[:, :HEAD_DIM])
            m = jnp.max(logits, axis=-1, keepdims=True)
            finish(rows, _dot(jnp.exp2(logits - m).astype(BF16), mv_ref[...]))


def _mem_attn(main, mkn, mv, gq, gk, ong, batch, seq, n_mem):
    t = main.shape[0]
    cb = lambda off: off // HEAD_DIM
    return pl.pallas_call(
        _memattn_kernel,
        grid=(batch, MEM_HEADS),
        in_specs=[
            pl.BlockSpec((seq, HEAD_DIM), lambda b, h: (b, cb(OFF_MQ) + h)),
            pl.BlockSpec((seq, HEAD_DIM), lambda b, h: (b, cb(OFF_MG) + h)),
            pl.BlockSpec((n_mem, 2 * HEAD_DIM), lambda b, h: (b, h)),
            pl.BlockSpec((n_mem, 2 * HEAD_DIM), lambda b, h: (b, h)),
            pl.BlockSpec((1, HEAD_DIM), lambda b, h: (0, 0)),
            pl.BlockSpec((1, HEAD_DIM), lambda b, h: (0, 0)),
            pl.BlockSpec((None, 1, HEAD_DIM), lambda b, h: (h, 0, 0)),
        ],
        out_specs=pl.BlockSpec((seq, HEAD_DIM), lambda b, h: (b, h)),
        out_shape=jax.ShapeDtypeStruct((t, MEM_HEADS * HEAD_DIM), BF16),
        compiler_params=_params(("parallel", "arbitrary")),
        name="mem_attn",
    )(main, main, mkn, mv, gq, gk, ong)


def _outproj_kernel(og_ref, of_ref, om_ref, x_ref, wo_ref, gm_ref, h_ref, hn_ref):
    n_g = og_ref.shape[1]
    n_f = of_ref.shape[1]
    acc = _dot(og_ref[...], wo_ref[:n_g, :])
    acc = acc + _dot(of_ref[...], wo_ref[n_g:n_g + n_f, :])
    acc = acc + _dot(om_ref[...], wo_ref[n_g + n_f:, :])
    h = x_ref[...] + acc
    h_ref[...] = h
    hn_ref[...] = _rms(h, gm_ref[...]).astype(BF16)


def _out_proj(og, of, om, x2, wo, gm, tm=512):
    t, d = x2.shape
    row = lambda i: (i, 0)
    fixed = lambda i: (0, 0)
    return pl.pallas_call(
        _outproj_kernel,
        grid=(t // tm,),
        in_specs=[
            pl.BlockSpec((tm, og.shape[1]), row),
            pl.BlockSpec((tm, of.shape[1]), row),
            pl.BlockSpec((tm, om.shape[1]), row),
            pl.BlockSpec((tm, d), row),
            pl.BlockSpec(wo.shape, fixed),
            pl.BlockSpec((1, d), fixed),
        ],
        out_specs=[pl.BlockSpec((tm, d), row), pl.BlockSpec((tm, d), row)],
        out_shape=[jax.ShapeDtypeStruct((t, d), F32), jax.ShapeDtypeStruct((t, d), BF16)],
        compiler_params=_params(("parallel",)),
        name="out_proj",
    )(og, of, om, x2, wo, gm)


def _mlp_kernel(hn_ref, h_ref, wu_ref, wd_ref, o_ref):
    @pl.when(pl.program_id(1) == 0)
    def _():
        o_ref[...] = h_ref[...]

    u = jnp.maximum(_dot(hn_ref[...], wu_ref[...]), 0.0)
    o_ref[...] += _dot((u * u).astype(BF16), wd_ref[...])


def _mlp(hn, h, wu, wd, tm=512, tf=1024):
    t, d = h.shape
    dff = wu.shape[1]
    return pl.pallas_call(
        _mlp_kernel,
        grid=(t // tm, dff // tf),
        in_specs=[
            pl.BlockSpec((tm, d), lambda i, f: (i, 0)),
            pl.BlockSpec((tm, d), lambda i, f: (i, 0)),
            pl.BlockSpec((d, tf), lambda i, f: (0, f)),
            pl.BlockSpec((tf, d), lambda i, f: (f, 0)),
        ],
        out_specs=pl.BlockSpec((tm, d), lambda i, f: (i, 0)),
        out_shape=jax.ShapeDtypeStruct((t, d), F32),
        compiler_params=_params(("parallel", "arbitrary")),
        name="mlp",
    )(hn, h, wu, wd)


def _regroup_w_in(w_in):
    n_gla = OFF_FQ
    n_fox = OFF_MQ - OFF_FQ
    a0 = n_gla
    f0 = a0 + GLA_RANK
    ff0 = f0 + n_fox
    m0 = ff0 + FOX_HEADS
    wb = w_in.astype(BF16)
    main = jnp.concatenate([wb[:, :n_gla], wb[:, f0:ff0], wb[:, m0:]], axis=1)
    pad = jnp.zeros((w_in.shape[0], LANES - GLA_RANK - FOX_HEADS), BF16)
    small = jnp.concatenate([wb[:, a0:f0], wb[:, ff0:m0], pad], axis=1)
    return main, small


def kernel(x, mem, attn_norm_g, w_in, gla_a_w2, gla_a_b, fox_f_b, fox_q_norm_g, fox_k_norm_g,
           mem_norm_g, w_mem_kv, mem_q_norm_g, mem_k_norm_g, out_norm_g, w_out, mlp_norm_g,
           w_up, w_down):
    batch, seq, d = x.shape
    n_mem = mem.shape[1]
    t = batch * seq
    x2 = x.reshape(t, d)
    row = lambda a: a.reshape(1, -1).astype(F32)

    w_main, w_small = _regroup_w_in(w_in)
    main, small = _in_proj(x2, row(attn_norm_g), w_main, w_small)

    n_gla = GLA_HEADS * HEAD_DIM
    n_fox = FOX_HEADS * HEAD_DIM
    w2p = jnp.zeros((LANES, GLA_HEADS * GLA_DK), F32).at[:GLA_RANK].set(gla_a_w2).astype(BF16)
    og = _gla(main, small, w2p, row(gla_a_b), row(out_norm_g[:n_gla]), batch, seq)

    fb = jnp.zeros((1, LANES), F32).at[0, SMALL_FF:SMALL_FF + FOX_HEADS].set(fox_f_b)
    crow, aq, ak = _fox_gate(small, fb, row(fox_q_norm_g), row(fox_k_norm_g), batch, seq)
    ong_fox = out_norm_g[n_gla:n_gla + n_fox].reshape(FOX_HEADS, 1, HEAD_DIM)
    of = _fox(main, aq, ak, crow.reshape(batch, SUBLANES, 1, seq), row(fox_q_norm_g),
              row(fox_k_norm_g), ong_fox, batch, seq)

    mkn, mv = _mem_kv(mem.reshape(batch * n_mem, d), row(mem_norm_g), w_mem_kv.astype(BF16),
                      row(mem_q_norm_g), row(mem_k_norm_g), batch, n_mem)
    ong_mem = out_norm_g[n_gla + n_fox:].reshape(MEM_HEADS, 1, HEAD_DIM)
    om = _mem_attn(main, mkn, mv, row(mem_q_norm_g), row(mem_k_norm_g), ong_mem, batch, seq, n_mem)

    h, hn = _out_proj(og, of, om, x2, w_out.astype(BF16), row(mlp_norm_g))
    y = _mlp(hn, h, w_up.astype(BF16), w_down.astype(BF16))
    return y.reshape(batch, seq, d)
```

```python
import functools
import math

import jax
import jax.numpy as jnp
import numpy as np
from jax import lax
from jax.experimental import pallas as pl
from jax.experimental.pallas import tpu as pltpu

F32 = jnp.float32
BF16 = jnp.bfloat16

HEAD_DIM = 128
GLA_HEADS = 8
GLA_DK = 64
GLA_RANK = 16
GLA_TAU = 16.0
FOX_HEADS = 4
MEM_HEADS = 4
EPS = 1e-6

LANES = 128
SUBLANES = 8
LOG2E = 1.4426950408889634
GLA_CHUNK = 64
GLA_STEP = 512
GLA_CUMSUM_BLOCK = 256
GLA_SAFE_EXP = 86.0
FOX_TQ = 256
FOX_AUG = 8
FOX_MAX_SPAN = 100.0
MEM_TQ = 512
OUT_PROJ_SLABS = 4
NEG_BIG = -1e30
VMEM_LIMIT = 50 * 1024 * 1024

OFF_GQ, OFF_GK, OFF_GV, OFF_GG = 0, 512, 1024, 2048
OFF_FQ, OFF_FK, OFF_FV, OFF_FG = 3072, 3584, 4096, 4608
OFF_MQ, OFF_MG = 5120, 5632
N_MAIN = 6144
SMALL_FF = GLA_RANK
FF_ROW0 = SMALL_FF
assert FF_ROW0 % SUBLANES == 0 and FOX_HEADS <= SUBLANES


def _dot(a, b):
    return jnp.dot(a, b, preferred_element_type=F32)


def _dot_nt(a, b):
    return lax.dot_general(a, b, (((1,), (1,)), ((), ())), preferred_element_type=F32)


def _dot_tn(a, b):
    return lax.dot_general(a, b, (((0,), (0,)), ((), ())), preferred_element_type=F32)


def _rms(x, g):
    return x * lax.rsqrt(jnp.mean(x * x, axis=-1, keepdims=True) + EPS) * g


def _log_sigmoid(z):
    return jnp.minimum(z, 0.0) - jnp.log(1.0 + jnp.exp(-jnp.abs(z)))


def _sigmoid(z):
    return 1.0 / (1.0 + jnp.exp(-z))


def _split3(a):
    hi = a.astype(BF16)
    r = a - hi.astype(F32)
    mid = r.astype(BF16)
    lo = (r - mid.astype(F32)).astype(BF16)
    return hi, mid, lo


def _params(sem):
    return pltpu.CompilerParams(dimension_semantics=sem, vmem_limit_bytes=VMEM_LIMIT)


def _inproj_kernel(x_ref, g_ref, wm_ref, ws_ref, main_ref, small_ref, xn_even, xn_odd):
    i = pl.program_id(0)
    j = pl.program_id(1)
    n_col = pl.num_programs(1)
    rs = x_ref.shape[0] // n_col
    odd = i % 2 == 1

    def normalise_slice(dst):
        rows = pl.ds(pl.multiple_of(j * rs, rs), rs)
        dst[rows, :] = _rms(x_ref[rows, :], g_ref[...]).astype(BF16)

    def step(src, dst):
        @pl.when(j == 0)
        def _():
            small_ref[...] = _dot(src[...], ws_ref[...])

        normalise_slice(dst)
        main_ref[...] = _dot(src[...], wm_ref[...]).astype(BF16)

    @pl.when(i == 0)
    def _():
        normalise_slice(xn_even)

    @pl.when(odd)
    def _():
        step(xn_even, xn_odd)

    @pl.when((i > 0) & jnp.logical_not(odd))
    def _():
        step(xn_odd, xn_even)


def _in_proj(x2, g, w_main, w_small, tm=1024, tn=1536):
    t, d = x2.shape
    n_row = t // tm
    prev = lambda i: jnp.maximum(i - 1, 0)
    return pl.pallas_call(
        _inproj_kernel,
        grid=(n_row + 1, N_MAIN // tn),
        in_specs=[
            pl.BlockSpec((tm, d), lambda i, j: (jnp.minimum(i, n_row - 1), 0)),
            pl.BlockSpec((1, d), lambda i, j: (0, 0)),
            pl.BlockSpec((d, tn), lambda i, j: (0, j)),
            pl.BlockSpec((d, LANES), lambda i, j: (0, 0)),
        ],
        out_specs=[
            pl.BlockSpec((tm, tn), lambda i, j: (prev(i), jnp.where(i > 0, j, 0))),
            pl.BlockSpec((tm, LANES), lambda i, j: (prev(i), 0)),
        ],
        out_shape=[
            jax.ShapeDtypeStruct((t, N_MAIN), BF16),
            jax.ShapeDtypeStruct((t, LANES), F32),
        ],
        scratch_shapes=[pltpu.VMEM((tm, d), BF16), pltpu.VMEM((tm, d), BF16)],
        compiler_params=_params(("arbitrary", "arbitrary")),
        name="in_proj",
    )(x2, g, w_main, w_small)


def _gla_kernel(q_ref, k_ref, v_ref, gg_ref, small_ref, small_next_ref, w2_ref, ab_ref, ong_ref,
                o_ref, st_ref, bc_scr, safe_ref, qs_scr, bs_scr, os_scr):
    C = GLA_CHUNK
    n_pairs = GLA_HEADS // 2
    step = pl.program_id(0) * pl.num_programs(1) + pl.program_id(1)
    slot = step % 2

    @pl.when(pl.program_id(1) == 0)
    def _():
        st_ref[...] = jnp.zeros_like(st_ref)

    def decay_sums(src_ref, dst):
        z = _dot(src_ref[...].astype(BF16), w2_ref[...]) + ab_ref[...]
        la = _log_sigmoid(z) * (LOG2E / GLA_TAU)
        blk = GLA_CUMSUM_BLOCK
        ri = lax.broadcasted_iota(jnp.int32, (blk, blk), 0)
        ci = lax.broadcasted_iota(jnp.int32, (blk, blk), 1)
        same_chunk = (ri ^ ci) < C
        tri = jnp.where(same_chunk, jnp.where(ci <= ri, 1.0, 0.0), 0.0).astype(BF16)
        for i in range(GLA_STEP // blk):
            rows = slice(i * blk, (i + 1) * blk)
            hi, mid, lo = _split3(la[rows, :])
            bc_scr[dst, rows, :] = _dot(tri, hi) + _dot(tri, mid) + _dot(tri, lo)
        safe_ref[dst] = (jnp.min(la) * C >= -GLA_SAFE_EXP).astype(jnp.int32)

    @pl.when(step == 0)
    def _():
        decay_sums(small_ref, 0)

    lane = lax.broadcasted_iota(jnp.int32, (C, LANES), 1)
    low_half = lane < GLA_DK
    st_r = lax.broadcasted_iota(jnp.int32, (2 * HEAD_DIM, LANES), 0)
    st_c = lax.broadcasted_iota(jnp.int32, (2 * HEAD_DIM, LANES), 1)
    st_mask = (st_r < HEAD_DIM) == (st_c < GLA_DK)
    a_r = lax.broadcasted_iota(jnp.int32, (C, 2 * C), 0)
    a_c = lax.broadcasted_iota(jnp.int32, (C, 2 * C), 1)
    causal = a_r >= (a_c & (C - 1))
    zeros_v = jnp.zeros((C, HEAD_DIM), BF16)
    jrow = lax.broadcasted_iota(jnp.int32, (C, LANES), 0)

    def unit_first_half(ch, p, fast):
        rows = slice(ch * C, (ch + 1) * C)
        kl = slice(p * LANES, (p + 1) * LANES)
        vl = slice(p * 2 * HEAD_DIM, (p + 1) * 2 * HEAD_DIM)
        b = bc_scr[slot, rows, kl]
        qs = q_ref[rows, kl].astype(F32)
        kf = k_ref[rows, kl].astype(F32)
        v2 = v_ref[rows, vl]
        qe16 = (qs * jnp.exp2(b)).astype(BF16)
        b_last = b[C - 1:C, :]
        st = st_ref[p]
        o = _dot_nt(qe16, st.astype(BF16))
        if fast:
            ke = kf * jnp.exp2(-b)
            kbd = jnp.concatenate(
                [jnp.where(low_half, ke, 0.0), jnp.where(low_half, 0.0, ke)], axis=0).astype(BF16)
            scores = _dot_nt(qe16, kbd)
            vbd = jnp.concatenate(
                [jnp.concatenate([v2[:, :HEAD_DIM], zeros_v], axis=1),
                 jnp.concatenate([zeros_v, v2[:, HEAD_DIM:]], axis=1)], axis=0)
            upd = _dot_tn(vbd, kbd)
            st_ref[p] = jnp.exp2(b_last) * (st + upd)
            return rows, p, o, scores, vbd
        else:
            kd16 = (kf * jnp.exp2(b_last - b)).astype(BF16)
            upd = _dot_tn(v2, kd16)
            st_ref[p] = jnp.exp2(b_last) * st + jnp.where(st_mask, upd, 0.0)
            qs_scr[...] = qs
            bs_scr[...] = b
            v2f = v2.astype(F32)

            def row(i, carry):
                qi = qs_scr[pl.ds(i, 1), :]
                bi = bs_scr[pl.ds(i, 1), :]
                w = qi * kf * jnp.exp2(jnp.minimum(bi - b, 0.0))
                w = jnp.where(jrow <= i, w, 0.0)
                p0 = jnp.sum(jnp.where(low_half, w, 0.0), axis=1, keepdims=True)
                p1 = jnp.sum(jnp.where(low_half, 0.0, w), axis=1, keepdims=True)
                pv = jnp.concatenate([p0 * v2f[:, :HEAD_DIM], p1 * v2f[:, HEAD_DIM:]], axis=1)
                os_scr[pl.ds(i, 1), :] = jnp.sum(pv, axis=0, keepdims=True)
                return carry

            lax.fori_loop(0, C, row, 0)
            return rows, p, o + os_scr[...], None, None

    def unit_second_half(rows, p, o, scores, vbd):
        if scores is not None:
            o = o + _dot(jnp.where(causal, scores, 0.0).astype(BF16), vbd)
        for hh in range(2):
            hl = slice((2 * p + hh) * HEAD_DIM, (2 * p + hh + 1) * HEAD_DIM)
            on = _rms(o[:, hh * HEAD_DIM:(hh + 1) * HEAD_DIM], ong_ref[:, hl])
            g = gg_ref[rows, hl].astype(F32)
            o_ref[rows, hl] = (on * (g * _sigmoid(g))).astype(BF16)

    def run(fast):
        n_chunks = GLA_STEP // C
        for ch in range(n_chunks):
            if ch == n_chunks // 2:
                decay_sums(small_next_ref, 1 - slot)
            halves = [unit_first_half(ch, p, fast) for p in range(n_pairs)]
            for half in halves:
                unit_second_half(*half)

    safe = safe_ref[slot]

    @pl.when(safe == 1)
    def _():
        run(True)

    @pl.when(safe == 0)
    def _():
        run(False)


def _gla(main, small, w2p, ab, ong, batch, seq):
    t = main.shape[0]
    nc = seq // GLA_STEP
    rb = lambda b, c: b * nc + c
    rb_next = lambda b, c: jnp.minimum(b * nc + c + 1, batch * nc - 1)
    dq = GLA_HEADS * GLA_DK
    dv = GLA_HEADS * HEAD_DIM
    return pl.pallas_call(
        _gla_kernel,
        grid=(batch, nc),
        in_specs=[
            pl.BlockSpec((GLA_STEP, dq), lambda b, c: (rb(b, c), OFF_GQ // dq)),
            pl.BlockSpec((GLA_STEP, dq), lambda b, c: (rb(b, c), OFF_GK // dq)),
            pl.BlockSpec((GLA_STEP, dv), lambda b, c: (rb(b, c), OFF_GV // dv)),
            pl.BlockSpec((GLA_STEP, dv), lambda b, c: (rb(b, c), OFF_GG // dv)),
            pl.BlockSpec((GLA_STEP, LANES), lambda b, c: (rb(b, c), 0)),
            pl.BlockSpec((GLA_STEP, LANES), lambda b, c: (rb_next(b, c), 0)),
            pl.BlockSpec((LANES, dq), lambda b, c: (0, 0)),
            pl.BlockSpec((1, dq), lambda b, c: (0, 0)),
            pl.BlockSpec((1, dv), lambda b, c: (0, 0)),
        ],
        out_specs=pl.BlockSpec((GLA_STEP, dv), lambda b, c: (rb(b, c), 0)),
        out_shape=jax.ShapeDtypeStruct((t, dv), BF16),
        scratch_shapes=[
            pltpu.VMEM((GLA_HEADS // 2, 2 * HEAD_DIM, LANES), F32),
            pltpu.VMEM((2, GLA_STEP, dq), F32),
            pltpu.SMEM((2,), jnp.int32),
            pltpu.VMEM((GLA_CHUNK, LANES), F32),
            pltpu.VMEM((GLA_CHUNK, LANES), F32),
            pltpu.VMEM((GLA_CHUNK, 2 * HEAD_DIM), F32),
        ],
        compiler_params=_params(("arbitrary", "arbitrary")),
        name="gla",
    )(main, main, main, main, small, small, w2p, ab, ong)


def _foxgate_kernel(small_ref, fb_ref, gq_ref, gk_ref, p_ref, rows_ref,
                    crow_ref, aq_ref, ak_ref, c_scr):
    s = small_ref.shape[0]
    blk = 256
    ri = lax.broadcasted_iota(jnp.int32, (blk, blk), 0)
    ci = lax.broadcasted_iota(jnp.int32, (blk, blk), 1)
    tri = jnp.where(ci <= ri, 1.0, 0.0).astype(BF16)
    carry = jnp.zeros((1, LANES), F32)
    qk_bound = _qk_bound(gq_ref[...], gk_ref[...])
    b_hi = qk_bound.astype(BF16).astype(F32)
    b_lo = (qk_bound - b_hi).astype(BF16).astype(F32)
    q_const = rows_ref[0:1, :]
    k_const = rows_ref[1:2, :] + b_hi * rows_ref[2:3, :] + b_lo * rows_ref[3:4, :]
    for i in range(s // blk):
        rows = slice(i * blk, (i + 1) * blk)
        lf = _log_sigmoid(small_ref[rows, :] + fb_ref[...]) * LOG2E
        hi, mid, lo = _split3(lf)
        c = _dot(tri, hi) + _dot(tri, mid) + _dot(tri, lo) + carry
        c_scr[rows, :] = c
        carry = c[blk - 1:blk, :]
        aug = _dot(jnp.concatenate(_split3(c), axis=1), p_ref[...])
        aq_ref[rows, :] = (aug[:, :LANES] + q_const).astype(BF16)
        ak_ref[rows, :] = (aug[:, LANES:] + k_const).astype(BF16)
    crow_ref[...] = c_scr[...].T[FF_ROW0:FF_ROW0 + SUBLANES, :]


def _qk_bound(gq, gk):
    gmax = jnp.max(jnp.abs(gq), axis=-1, keepdims=True) * jnp.max(jnp.abs(gk), axis=-1, keepdims=True)
    return gmax * (1.01 * LOG2E * HEAD_DIM ** 0.5)


def _fox_aug_constants():
    p = np.zeros((3 * LANES, 2 * LANES), np.float32)
    rows = np.zeros((SUBLANES, LANES), np.float32)
    for h in range(FOX_HEADS):
        base = FOX_AUG * h
        for j in range(3):
            p[j * LANES + SMALL_FF + h, base + j] = 1.0
            p[j * LANES + SMALL_FF + h, LANES + base + 3 + j] = -1.0
            rows[1, base + j] = 1.0
        rows[0, base + 3:base + 8] = 1.0
        rows[2, base + 6] = -1.0
        rows[3, base + 7] = -1.0
    return jnp.asarray(p, BF16), jnp.asarray(rows)


def _fox_gate(small, fb, gq, gk, batch, seq):
    t = small.shape[0]
    p, const_rows = _fox_aug_constants()
    fixed = lambda b: (0, 0)
    return pl.pallas_call(
        _foxgate_kernel,
        grid=(batch,),
        in_specs=[
            pl.BlockSpec((seq, LANES), lambda b: (b, 0)),
            pl.BlockSpec((1, LANES), fixed),
            pl.BlockSpec((1, HEAD_DIM), fixed),
            pl.BlockSpec((1, HEAD_DIM), fixed),
            pl.BlockSpec(p.shape, fixed),
            pl.BlockSpec(const_rows.shape, fixed),
        ],
        out_specs=[
            pl.BlockSpec((None, SUBLANES, seq), lambda b: (b, 0, 0)),
            pl.BlockSpec((seq, LANES), lambda b: (b, 0)),
            pl.BlockSpec((seq, LANES), lambda b: (b, 0)),
        ],
        out_shape=[
            jax.ShapeDtypeStruct((batch, SUBLANES, seq), F32),
            jax.ShapeDtypeStruct((t, LANES), BF16),
            jax.ShapeDtypeStruct((t, LANES), BF16),
        ],
        scratch_shapes=[pltpu.VMEM((seq, LANES), F32)],
        compiler_params=_params(("parallel",)),
        name="fox_gate",
    )(small, fb, gq, gk, p, const_rows)


def _fox_kernel(q_ref, k_ref, v_ref, fg_ref, aq_ref, ak_ref, crow_ref, gq_ref, gk_ref, ong_ref,
                o_ref, ka_scr, v1_scr):
    s = q_ref.shape[0]
    h = pl.program_id(1)
    scale = HEAD_DIM ** -0.5 * LOG2E
    lane = lax.broadcasted_iota(jnp.int32, (FOX_TQ, LANES), 1)
    own = (lane >= FOX_AUG * h) & (lane < FOX_AUG * (h + 1))
    qi = lax.broadcasted_iota(jnp.int32, (FOX_TQ, FOX_TQ), 0)
    ki = lax.broadcasted_iota(jnp.int32, (FOX_TQ, FOX_TQ), 1)
    causal = qi >= ki
    ones = jnp.ones((FOX_TQ, HEAD_DIM), BF16)

    def prepare(rows):
        qn = (_rms(q_ref[rows, :].astype(F32), gq_ref[...]) * scale).astype(BF16)
        kn = _rms(k_ref[rows, :].astype(F32), gk_ref[...]).astype(BF16)
        qa = jnp.concatenate(
            [qn, jnp.where(own, aq_ref[rows, :].astype(F32), 0.0).astype(BF16)], axis=1)
        ka = jnp.concatenate([kn, ak_ref[rows, :]], axis=1)
        v1 = jnp.concatenate([v_ref[rows, :], ones], axis=1)
        ka_scr[rows, :] = ka
        v1_scr[rows, :] = v1
        return qa, ka, v1

    def finish(rows, ol):
        o = ol[:, :HEAD_DIM] / ol[:, HEAD_DIM:]
        on = _rms(o, ong_ref[...])
        o_ref[rows, :] = (on * _sigmoid(fg_ref[rows, :].astype(F32))).astype(BF16)

    bounded = 2.0 * _qk_bound(gq_ref[...], gk_ref[...])[0, 0] <= FOX_MAX_SPAN

    @pl.when(bounded)
    def _():
        def logits(qb):
            r0 = qb * FOX_TQ
            rows = slice(r0, r0 + FOX_TQ)
            qa, ka, v1 = prepare(rows)
            diag = jnp.where(causal, _dot_nt(qa, ka), NEG_BIG)
            past = _dot_nt(qa, ka_scr[:r0, :]) if r0 else None
            return rows, r0, diag, past, v1

        def weighted_values(rows, r0, diag, past, v1):
            ol = _dot(jnp.exp2(diag).astype(BF16), v1)
            if r0:
                ol = ol + _dot(jnp.exp2(past).astype(BF16), v1_scr[:r0, :])
            finish(rows, ol)

        n_blocks = s // FOX_TQ
        pending = logits(0)
        for qb in range(n_blocks):
            nxt = logits(qb + 1) if qb + 1 < n_blocks else None
            weighted_values(*pending)
            pending = nxt

    @pl.when(jnp.logical_not(bounded))
    def _():
        for qb in range(s // FOX_TQ):
            r0 = qb * FOX_TQ
            rows = slice(r0, r0 + FOX_TQ)
            qa, ka, v1 = prepare(rows)
            qn = qa[:, :HEAD_DIM]
            diag = _dot_nt(qn, ka[:, :HEAD_DIM]) - crow_ref[:, rows]
            diag = jnp.where(causal, diag, NEG_BIG)
            m = jnp.max(diag, axis=-1, keepdims=True)
            if r0:
                past = _dot_nt(qn, ka_scr[:r0, :HEAD_DIM]) - crow_ref[:, :r0]
                m = jnp.maximum(m, jnp.max(past, axis=-1, keepdims=True))
                ol = _dot(jnp.exp2(past - m).astype(BF16), v1_scr[:r0, :])
                ol = ol + _dot(jnp.exp2(diag - m).astype(BF16), v1)
            else:
                ol = _dot(jnp.exp2(diag - m).astype(BF16), v1)
            finish(rows, ol)


def _fox(main, aq, ak, crow4, gq, gk, ong, batch, seq):
    t = main.shape[0]
    cb = lambda off: off // HEAD_DIM
    return pl.pallas_call(
        _fox_kernel,
        grid=(batch, FOX_HEADS),
        in_specs=[
            pl.BlockSpec((seq, HEAD_DIM), lambda b, h: (b, cb(OFF_FQ) + h)),
            pl.BlockSpec((seq, HEAD_DIM), lambda b, h: (b, cb(OFF_FK) + h)),
            pl.BlockSpec((seq, HEAD_DIM), lambda b, h: (b, cb(OFF_FV) + h)),
            pl.BlockSpec((seq, HEAD_DIM), lambda b, h: (b, cb(OFF_FG) + h)),
            pl.BlockSpec((seq, LANES), lambda b, h: (b, 0)),
            pl.BlockSpec((seq, LANES), lambda b, h: (b, 0)),
            pl.BlockSpec((None, None, 1, seq), lambda b, h: (b, h, 0, 0)),
            pl.BlockSpec((1, HEAD_DIM), lambda b, h: (0, 0)),
            pl.BlockSpec((1, HEAD_DIM), lambda b, h: (0, 0)),
            pl.BlockSpec((None, 1, HEAD_DIM), lambda b, h: (h, 0, 0)),
        ],
        out_specs=pl.BlockSpec((seq, HEAD_DIM), lambda b, h: (b, h)),
        out_shape=jax.ShapeDtypeStruct((t, FOX_HEADS * HEAD_DIM), BF16),
        scratch_shapes=[
            pltpu.VMEM((seq, HEAD_DIM + LANES), BF16),
            pltpu.VMEM((seq, 2 * HEAD_DIM), BF16),
        ],
        compiler_params=_params(("parallel", "arbitrary")),
        name="fox",
    )(main, main, main, main, aq, ak, crow4, gq, gk, ong)


def _memkv_kernel(mem_ref, g_ref, w_ref, gq_ref, gk_ref, mk_ref, mv_ref):
    mn = _rms(mem_ref[...], g_ref[...]).astype(BF16)
    kv = _dot(mn, w_ref[...])
    n_mem = kv.shape[0]
    dk = MEM_HEADS * HEAD_DIM
    ones = jnp.ones((n_mem, HEAD_DIM), BF16)
    qk_bound = _qk_bound(gq_ref[...], gk_ref[...])
    b_hi = qk_bound.astype(BF16).astype(F32)
    b_lo = (qk_bound - b_hi).astype(BF16).astype(F32)
    lane = lax.broadcasted_iota(jnp.int32, (n_mem, LANES), 1)
    shift = jnp.where(lane == 0, -b_hi, jnp.where(lane == 1, -b_lo, 0.0)).astype(BF16)
    for hh in range(MEM_HEADS):
        hl = slice(hh * HEAD_DIM, (hh + 1) * HEAD_DIM)
        mk_ref[:, 2 * hh * HEAD_DIM:(2 * hh + 1) * HEAD_DIM] = _rms(kv[:, hl], gk_ref[...]).astype(BF16)
        mk_ref[:, (2 * hh + 1) * HEAD_DIM:(2 * hh + 2) * HEAD_DIM] = shift
        mv_ref[:, 2 * hh * HEAD_DIM:(2 * hh + 1) * HEAD_DIM] = kv[:, dk + hh * HEAD_DIM:dk + (hh + 1) * HEAD_DIM].astype(BF16)
        mv_ref[:, (2 * hh + 1) * HEAD_DIM:(2 * hh + 2) * HEAD_DIM] = ones


def _mem_kv(mem2, g, w, gq, gk, batch, n_mem):
    d = mem2.shape[1]
    dk = MEM_HEADS * HEAD_DIM
    return pl.pallas_call(
        _memkv_kernel,
        grid=(batch,),
        in_specs=[
            pl.BlockSpec((n_mem, d), lambda b: (b, 0)),
            pl.BlockSpec((1, d), lambda b: (0, 0)),
            pl.BlockSpec((d, 2 * dk), lambda b: (0, 0)),
            pl.BlockSpec((1, HEAD_DIM), lambda b: (0, 0)),
            pl.BlockSpec((1, HEAD_DIM), lambda b: (0, 0)),
        ],
        out_specs=[
            pl.BlockSpec((n_mem, 2 * dk), lambda b: (b, 0)),
            pl.BlockSpec((n_mem, 2 * dk), lambda b: (b, 0)),
        ],
        out_shape=[
            jax.ShapeDtypeStruct((batch * n_mem, 2 * dk), BF16),
            jax.ShapeDtypeStruct((batch * n_mem, 2 * dk), BF16),
        ],
        compiler_params=_params(("parallel",)),
        name="mem_kv",
    )(mem2, g, w, gq, gk)


def _memattn_kernel(q_ref, mg_ref, mk_ref, mv_ref, gq_ref, gk_ref, ong_ref, o_ref):
    s = q_ref.shape[0]
    scale = HEAD_DIM ** -0.5 * LOG2E
    lane = lax.broadcasted_iota(jnp.int32, (MEM_TQ, LANES), 1)
    q_extra = jnp.where(lane < 2, 1.0, 0.0).astype(BF16)

    def q_block(rows):
        return (_rms(q_ref[rows, :].astype(F32), gq_ref[...]) * scale).astype(BF16)

    def finish(rows, ol):
        o = ol[:, :HEAD_DIM] / ol[:, HEAD_DIM:]
        on = _rms(o, ong_ref[...])
        o_ref[rows, :] = (on * _sigmoid(mg_ref[rows, :].astype(F32))).astype(BF16)

    bounded = 2.0 * _qk_bound(gq_ref[...], gk_ref[...])[0, 0] <= FOX_MAX_SPAN

    @pl.when(bounded)
    def _():
        def logits(qb):
            rows = slice(qb * MEM_TQ, (qb + 1) * MEM_TQ)
            qa = jnp.concatenate([q_block(rows), q_extra], axis=1)
            return rows, _dot_nt(qa, mk_ref[...])

        n_blocks = s // MEM_TQ
        pending = logits(0)
        for qb in range(n_blocks):
            nxt = logits(qb + 1) if qb + 1 < n_blocks else None
            rows, lg = pending
            finish(rows, _dot(jnp.exp2(lg).astype(BF16), mv_ref[...]))
            pending = nxt

    @pl.when(jnp.logical_not(bounded))
    def _():
        for qb in range(s // MEM_TQ):
            rows = slice(qb * MEM_TQ, (qb + 1) * MEM_TQ)
            logits = _dot_nt(q_block(rows), mk_ref[:, :HEAD_DIM])
            m = jnp.max(logits, axis=-1, keepdims=True)
            finish(rows, _dot(jnp.exp2(logits - m).astype(BF16), mv_ref[...]))


def _mem_attn(main, mkn, mv, gq, gk, ong, batch, seq, n_mem):
    t = main.shape[0]
    cb = lambda off: off // HEAD_DIM
    return pl.pallas_call(
        _memattn_kernel,
        grid=(batch, MEM_HEADS),
        in_specs=[
            pl.BlockSpec((seq, HEAD_DIM), lambda b, h: (b, cb(OFF_MQ) + h)),
            pl.BlockSpec((seq, HEAD_DIM), lambda b, h: (b, cb(OFF_MG) + h)),
            pl.BlockSpec((n_mem, 2 * HEAD_DIM), lambda b, h: (b, h)),
            pl.BlockSpec((n_mem, 2 * HEAD_DIM), lambda b, h: (b, h)),
            pl.BlockSpec((1, HEAD_DIM), lambda b, h: (0, 0)),
            pl.BlockSpec((1, HEAD_DIM), lambda b, h: (0, 0)),
            pl.BlockSpec((None, 1, HEAD_DIM), lambda b, h: (h, 0, 0)),
        ],
        out_specs=pl.BlockSpec((seq, HEAD_DIM), lambda b, h: (b, h)),
        out_shape=jax.ShapeDtypeStruct((t, MEM_HEADS * HEAD_DIM), BF16),
        compiler_params=_params(("parallel", "arbitrary")),
        name="mem_attn",
    )(main, main, mkn, mv, gq, gk, ong)


def _outproj_kernel(og_ref, of_ref, om_ref, x_ref, wo_ref, gm_ref, h_ref, hn_ref):
    n_g = og_ref.shape[1]
    n_f = of_ref.shape[1]
    d = x_ref.shape[1]
    slab = d // OUT_PROJ_SLABS
    sumsq = jnp.zeros((x_ref.shape[0], 1), F32)
    for c in range(OUT_PROJ_SLABS):
        cols = slice(c * slab, (c + 1) * slab)
        acc = _dot(og_ref[...], wo_ref[:n_g, cols])
        acc = acc + _dot(of_ref[...], wo_ref[n_g:n_g + n_f, cols])
        acc = acc + _dot(om_ref[...], wo_ref[n_g + n_f:, cols])
        h = x_ref[:, cols] + acc
        h_ref[:, cols] = h
        sumsq = sumsq + jnp.sum(h * h, axis=-1, keepdims=True)
    inv = lax.rsqrt(sumsq * (1.0 / d) + EPS)
    hn_ref[...] = (h_ref[...] * inv * gm_ref[...]).astype(BF16)


def _out_proj(og, of, om, x2, wo, gm, tm=512):
    t, d = x2.shape
    row = lambda i: (i, 0)
    fixed = lambda i: (0, 0)
    return pl.pallas_call(
        _outproj_kernel,
        grid=(t // tm,),
        in_specs=[
            pl.BlockSpec((tm, og.shape[1]), row),
            pl.BlockSpec((tm, of.shape[1]), row),
            pl.BlockSpec((tm, om.shape[1]), row),
            pl.BlockSpec((tm, d), row),
            pl.BlockSpec(wo.shape, fixed),
            pl.BlockSpec((1, d), fixed),
        ],
        out_specs=[pl.BlockSpec((tm, d), row), pl.BlockSpec((tm, d), row)],
        out_shape=[jax.ShapeDtypeStruct((t, d), F32), jax.ShapeDtypeStruct((t, d), BF16)],
        compiler_params=_params(("parallel",)),
        name="out_proj",
    )(og, of, om, x2, wo, gm)


def _mlp_kernel(hn_ref, h_ref, wu_ref, wd_ref, o_ref):
    @pl.when(pl.program_id(1) == 0)
    def _():
        o_ref[...] = h_ref[...]

    u = jnp.maximum(_dot(hn_ref[...], wu_ref[...]), 0.0)
    o_ref[...] += _dot((u * u).astype(BF16), wd_ref[...])


def _mlp(hn, h, wu, wd, tm=512, tf=1024):
    t, d = h.shape
    dff = wu.shape[1]
    return pl.pallas_call(
        _mlp_kernel,
        grid=(t // tm, dff // tf),
        in_specs=[
            pl.BlockSpec((tm, d), lambda i, f: (i, 0)),
            pl.BlockSpec((tm, d), lambda i, f: (i, 0)),
            pl.BlockSpec((d, tf), lambda i, f: (0, f)),
            pl.BlockSpec((tf, d), lambda i, f: (f, 0)),
        ],
        out_specs=pl.BlockSpec((tm, d), lambda i, f: (i, 0)),
        out_shape=jax.ShapeDtypeStruct((t, d), F32),
        compiler_params=_params(("parallel", "arbitrary")),
        name="mlp",
    )(hn, h, wu, wd)


def _regroup_w_in(w_in):
    n_gla = OFF_FQ
    n_fox = OFF_MQ - OFF_FQ
    a0 = n_gla
    f0 = a0 + GLA_RANK
    ff0 = f0 + n_fox
    m0 = ff0 + FOX_HEADS
    wb = w_in.astype(BF16)
    q_scale = GLA_DK ** -0.5
    assert math.frexp(q_scale)[0] == 0.5
    n_gq = GLA_HEADS * GLA_DK
    main = jnp.concatenate(
        [wb[:, :n_gq] * q_scale, wb[:, n_gq:n_gla], wb[:, f0:ff0], wb[:, m0:]], axis=1)
    pad = jnp.zeros((w_in.shape[0], LANES - GLA_RANK - FOX_HEADS), BF16)
    small = jnp.concatenate([wb[:, a0:f0], wb[:, ff0:m0], pad], axis=1)
    return main, small


def kernel(x, mem, attn_norm_g, w_in, gla_a_w2, gla_a_b, fox_f_b, fox_q_norm_g, fox_k_norm_g,
           mem_norm_g, w_mem_kv, mem_q_norm_g, mem_k_norm_g, out_norm_g, w_out, mlp_norm_g,
           w_up, w_down):
    batch, seq, d = x.shape
    n_mem = mem.shape[1]
    t = batch * seq
    x2 = x.reshape(t, d)
    row = lambda a: a.reshape(1, -1).astype(F32)

    w_main, w_small = _regroup_w_in(w_in)
    main, small = _in_proj(x2, row(attn_norm_g), w_main, w_small)

    n_gla = GLA_HEADS * HEAD_DIM
    n_fox = FOX_HEADS * HEAD_DIM
    fb = jnp.zeros((1, LANES), F32).at[0, SMALL_FF:SMALL_FF + FOX_HEADS].set(fox_f_b)
    crow, aq, ak = _fox_gate(small, fb, row(fox_q_norm_g), row(fox_k_norm_g), batch, seq)
    ong_fox = out_norm_g[n_gla:n_gla + n_fox].reshape(FOX_HEADS, 1, HEAD_DIM)
    of = _fox(main, aq, ak, crow.reshape(batch, SUBLANES, 1, seq), row(fox_q_norm_g),
              row(fox_k_norm_g), ong_fox, batch, seq)

    mkn, mv = _mem_kv(mem.reshape(batch * n_mem, d), row(mem_norm_g), w_mem_kv.astype(BF16),
                      row(mem_q_norm_g), row(mem_k_norm_g), batch, n_mem)
    ong_mem = out_norm_g[n_gla + n_fox:].reshape(MEM_HEADS, 1, HEAD_DIM)
    om = _mem_attn(main, mkn, mv, row(mem_q_norm_g), row(mem_k_norm_g), ong_mem, batch, seq, n_mem)

    w2p = jnp.zeros((LANES, GLA_HEADS * GLA_DK), F32).at[:GLA_RANK].set(gla_a_w2).astype(BF16)
    og = _gla(main, small, w2p, row(gla_a_b), row(out_norm_g[:n_gla]), batch, seq)

    h, hn = _out_proj(og, of, om, x2, w_out.astype(BF16), row(mlp_norm_g))
    y = _mlp(hn, h, w_up.astype(BF16), w_down.astype(BF16))
    return y.reshape(batch, seq, d)
```

```python
import functools
import math

import jax
import jax.numpy as jnp
import numpy as np
from jax import lax
from jax.experimental import pallas as pl
from jax.experimental.pallas import tpu as pltpu

F32 = jnp.float32
BF16 = jnp.bfloat16

HEAD_DIM = 128
GLA_HEADS = 8
GLA_DK = 64
GLA_RANK = 16
GLA_TAU = 16.0
FOX_HEADS = 4
MEM_HEADS = 4
EPS = 1e-6

LANES = 128
SUBLANES = 8
LOG2E = 1.4426950408889634
GLA_CHUNK = 64
GLA_STEP = 512
GLA_CUMSUM_BLOCK = 256
GLA_SAFE_EXP = 86.0
FOX_TQ = 256
FOX_AUG = 8
FOX_MAX_SPAN = 100.0
MEM_TQ = 512
OUT_PROJ_SLABS = 4
NEG_BIG = -1e30
VMEM_LIMIT = 50 * 1024 * 1024
MLP_VMEM_TEMPS = 5 * 1024 * 1024
MLP_SPLIT = 2

OFF_GQ, OFF_GK, OFF_GV, OFF_GG = 0, 512, 1024, 2048
OFF_FQ, OFF_FK, OFF_FV, OFF_FG = 3072, 3584, 4096, 4608
OFF_MQ, OFF_MG = 5120, 5632
N_MAIN = 6144
SMALL_FF = GLA_RANK
FF_ROW0 = SMALL_FF
assert FF_ROW0 % SUBLANES == 0 and FOX_HEADS <= SUBLANES


def _dot(a, b):
    return jnp.dot(a, b, preferred_element_type=F32)


def _dot_nt(a, b):
    return lax.dot_general(a, b, (((1,), (1,)), ((), ())), preferred_element_type=F32)


def _dot_tn(a, b):
    return lax.dot_general(a, b, (((0,), (0,)), ((), ())), preferred_element_type=F32)


def _rms(x, g):
    return x * lax.rsqrt(jnp.mean(x * x, axis=-1, keepdims=True) + EPS) * g


def _log_sigmoid(z):
    return jnp.minimum(z, 0.0) - jnp.log(1.0 + jnp.exp(-jnp.abs(z)))


def _sigmoid(z):
    return 1.0 / (1.0 + jnp.exp(-z))


def _split3(a):
    hi = a.astype(BF16)
    r = a - hi.astype(F32)
    mid = r.astype(BF16)
    lo = (r - mid.astype(F32)).astype(BF16)
    return hi, mid, lo


def _params(sem, vmem_limit=VMEM_LIMIT):
    return pltpu.CompilerParams(dimension_semantics=sem, vmem_limit_bytes=vmem_limit)


def _inproj_kernel(x_ref, g_ref, wm_ref, ws_ref, main_ref, small_ref, xn_even, xn_odd):
    i = pl.program_id(0)
    j = pl.program_id(1)
    n_col = pl.num_programs(1)
    rs = x_ref.shape[0] // n_col
    odd = i % 2 == 1

    def normalise_slice(dst):
        rows = pl.ds(pl.multiple_of(j * rs, rs), rs)
        dst[rows, :] = _rms(x_ref[rows, :], g_ref[...]).astype(BF16)

    def step(src, dst):
        @pl.when(j == 0)
        def _():
            small_ref[...] = _dot(src[...], ws_ref[...])

        normalise_slice(dst)
        main_ref[...] = _dot(src[...], wm_ref[...]).astype(BF16)

    @pl.when(i == 0)
    def _():
        normalise_slice(xn_even)

    @pl.when(odd)
    def _():
        step(xn_even, xn_odd)

    @pl.when((i > 0) & jnp.logical_not(odd))
    def _():
        step(xn_odd, xn_even)


def _in_proj(x2, g, w_main, w_small, tm=1024, tn=1536):
    t, d = x2.shape
    n_row = t // tm
    prev = lambda i: jnp.maximum(i - 1, 0)
    return pl.pallas_call(
        _inproj_kernel,
        grid=(n_row + 1, N_MAIN // tn),
        in_specs=[
            pl.BlockSpec((tm, d), lambda i, j: (jnp.minimum(i, n_row - 1), 0)),
            pl.BlockSpec((1, d), lambda i, j: (0, 0)),
            pl.BlockSpec((d, tn), lambda i, j: (0, j)),
            pl.BlockSpec((d, LANES), lambda i, j: (0, 0)),
        ],
        out_specs=[
            pl.BlockSpec((tm, tn), lambda i, j: (prev(i), jnp.where(i > 0, j, 0))),
            pl.BlockSpec((tm, LANES), lambda i, j: (prev(i), 0)),
        ],
        out_shape=[
            jax.ShapeDtypeStruct((t, N_MAIN), BF16),
            jax.ShapeDtypeStruct((t, LANES), F32),
        ],
        scratch_shapes=[pltpu.VMEM((tm, d), BF16), pltpu.VMEM((tm, d), BF16)],
        compiler_params=_params(("arbitrary", "arbitrary")),
        name="in_proj",
    )(x2, g, w_main, w_small)


def _gla_kernel(q_ref, k_ref, v_ref, gg_ref, small_ref, small_next_ref, w2_ref, ab_ref, ong_ref,
                o_ref, st_ref, bc_scr, safe_ref, qs_scr, bs_scr, os_scr):
    C = GLA_CHUNK
    n_pairs = GLA_HEADS // 2
    step = pl.program_id(0) * pl.num_programs(1) + pl.program_id(1)
    slot = step % 2

    @pl.when(pl.program_id(1) == 0)
    def _():
        st_ref[...] = jnp.zeros_like(st_ref)

    def decay_sums(src_ref, dst):
        z = _dot(src_ref[...].astype(BF16), w2_ref[...]) + ab_ref[...]
        la = _log_sigmoid(z) * (LOG2E / GLA_TAU)
        blk = GLA_CUMSUM_BLOCK
        ri = lax.broadcasted_iota(jnp.int32, (blk, blk), 0)
        ci = lax.broadcasted_iota(jnp.int32, (blk, blk), 1)
        same_chunk = (ri ^ ci) < C
        tri = jnp.where(same_chunk, jnp.where(ci <= ri, 1.0, 0.0), 0.0).astype(BF16)
        for i in range(GLA_STEP // blk):
            rows = slice(i * blk, (i + 1) * blk)
            hi, mid, lo = _split3(la[rows, :])
            bc_scr[dst, rows, :] = _dot(tri, hi) + _dot(tri, mid) + _dot(tri, lo)
        safe_ref[dst] = (jnp.min(la) * C >= -GLA_SAFE_EXP).astype(jnp.int32)

    @pl.when(step == 0)
    def _():
        decay_sums(small_ref, 0)

    lane = lax.broadcasted_iota(jnp.int32, (C, LANES), 1)
    low_half = lane < GLA_DK
    st_r = lax.broadcasted_iota(jnp.int32, (2 * HEAD_DIM, LANES), 0)
    st_c = lax.broadcasted_iota(jnp.int32, (2 * HEAD_DIM, LANES), 1)
    st_mask = (st_r < HEAD_DIM) == (st_c < GLA_DK)
    a_r = lax.broadcasted_iota(jnp.int32, (C, 2 * C), 0)
    a_c = lax.broadcasted_iota(jnp.int32, (C, 2 * C), 1)
    causal = a_r >= (a_c & (C - 1))
    zeros_v = jnp.zeros((C, HEAD_DIM), BF16)
    jrow = lax.broadcasted_iota(jnp.int32, (C, LANES), 0)

    def unit_first_half(ch, p, fast):
        rows = slice(ch * C, (ch + 1) * C)
        kl = slice(p * LANES, (p + 1) * LANES)
        vl = slice(p * 2 * HEAD_DIM, (p + 1) * 2 * HEAD_DIM)
        b = bc_scr[slot, rows, kl]
        qs = q_ref[rows, kl].astype(F32)
        kf = k_ref[rows, kl].astype(F32)
        v2 = v_ref[rows, vl]
        qe16 = (qs * jnp.exp2(b)).astype(BF16)
        b_last = b[C - 1:C, :]
        st = st_ref[p]
        o = _dot_nt(qe16, st.astype(BF16))
        if fast:
            ke = kf * jnp.exp2(-b)
            kbd = jnp.concatenate(
                [jnp.where(low_half, ke, 0.0), jnp.where(low_half, 0.0, ke)], axis=0).astype(BF16)
            scores = _dot_nt(qe16, kbd)
            vbd = jnp.concatenate(
                [jnp.concatenate([v2[:, :HEAD_DIM], zeros_v], axis=1),
                 jnp.concatenate([zeros_v, v2[:, HEAD_DIM:]], axis=1)], axis=0)
            upd = _dot_tn(vbd, kbd)
            st_ref[p] = jnp.exp2(b_last) * (st + upd)
            return rows, p, o, scores, vbd
        else:
            kd16 = (kf * jnp.exp2(b_last - b)).astype(BF16)
            upd = _dot_tn(v2, kd16)
            st_ref[p] = jnp.exp2(b_last) * st + jnp.where(st_mask, upd, 0.0)
            qs_scr[...] = qs
            bs_scr[...] = b
            v2f = v2.astype(F32)

            def row(i, carry):
                qi = qs_scr[pl.ds(i, 1), :]
                bi = bs_scr[pl.ds(i, 1), :]
                w = qi * kf * jnp.exp2(jnp.minimum(bi - b, 0.0))
                w = jnp.where(jrow <= i, w, 0.0)
                p0 = jnp.sum(jnp.where(low_half, w, 0.0), axis=1, keepdims=True)
                p1 = jnp.sum(jnp.where(low_half, 0.0, w), axis=1, keepdims=True)
                pv = jnp.concatenate([p0 * v2f[:, :HEAD_DIM], p1 * v2f[:, HEAD_DIM:]], axis=1)
                os_scr[pl.ds(i, 1), :] = jnp.sum(pv, axis=0, keepdims=True)
                return carry

            lax.fori_loop(0, C, row, 0)
            return rows, p, o + os_scr[...], None, None

    def unit_second_half(rows, p, o, scores, vbd):
        if scores is not None:
            o = o + _dot(jnp.where(causal, scores, 0.0).astype(BF16), vbd)
        for hh in range(2):
            hl = slice((2 * p + hh) * HEAD_DIM, (2 * p + hh + 1) * HEAD_DIM)
            on = _rms(o[:, hh * HEAD_DIM:(hh + 1) * HEAD_DIM], ong_ref[:, hl])
            g = gg_ref[rows, hl].astype(F32)
            o_ref[rows, hl] = (on * (g * _sigmoid(g))).astype(BF16)

    def run(fast):
        n_chunks = GLA_STEP // C
        for ch in range(n_chunks):
            if ch == n_chunks // 2:
                decay_sums(small_next_ref, 1 - slot)
            for p in range(n_pairs):
                unit_second_half(*unit_first_half(ch, p, fast))

    safe = safe_ref[slot]

    @pl.when(safe == 1)
    def _():
        run(True)

    @pl.when(safe == 0)
    def _():
        run(False)


def _gla(main, small, w2p, ab, ong, batch, seq):
    t = main.shape[0]
    nc = seq // GLA_STEP
    rb = lambda b, c: b * nc + c
    rb_next = lambda b, c: jnp.minimum(b * nc + c + 1, batch * nc - 1)
    dq = GLA_HEADS * GLA_DK
    dv = GLA_HEADS * HEAD_DIM
    return pl.pallas_call(
        _gla_kernel,
        grid=(batch, nc),
        in_specs=[
            pl.BlockSpec((GLA_STEP, dq), lambda b, c: (rb(b, c), OFF_GQ // dq)),
            pl.BlockSpec((GLA_STEP, dq), lambda b, c: (rb(b, c), OFF_GK // dq)),
            pl.BlockSpec((GLA_STEP, dv), lambda b, c: (rb(b, c), OFF_GV // dv)),
            pl.BlockSpec((GLA_STEP, dv), lambda b, c: (rb(b, c), OFF_GG // dv)),
            pl.BlockSpec((GLA_STEP, LANES), lambda b, c: (rb(b, c), 0)),
            pl.BlockSpec((GLA_STEP, LANES), lambda b, c: (rb_next(b, c), 0)),
            pl.BlockSpec((LANES, dq), lambda b, c: (0, 0)),
            pl.BlockSpec((1, dq), lambda b, c: (0, 0)),
            pl.BlockSpec((1, dv), lambda b, c: (0, 0)),
        ],
        out_specs=pl.BlockSpec((GLA_STEP, dv), lambda b, c: (rb(b, c), 0)),
        out_shape=jax.ShapeDtypeStruct((t, dv), BF16),
        scratch_shapes=[
            pltpu.VMEM((GLA_HEADS // 2, 2 * HEAD_DIM, LANES), F32),
            pltpu.VMEM((2, GLA_STEP, dq), F32),
            pltpu.SMEM((2,), jnp.int32),
            pltpu.VMEM((GLA_CHUNK, LANES), F32),
            pltpu.VMEM((GLA_CHUNK, LANES), F32),
            pltpu.VMEM((GLA_CHUNK, 2 * HEAD_DIM), F32),
        ],
        compiler_params=_params(("arbitrary", "arbitrary")),
        name="gla",
    )(main, main, main, main, small, small, w2p, ab, ong)


def _foxgate_kernel(small_ref, fb_ref, gq_ref, gk_ref, p_ref, rows_ref,
                    crow_ref, aq_ref, ak_ref, c_scr):
    s = small_ref.shape[0]
    blk = 256
    ri = lax.broadcasted_iota(jnp.int32, (blk, blk), 0)
    ci = lax.broadcasted_iota(jnp.int32, (blk, blk), 1)
    tri = jnp.where(ci <= ri, 1.0, 0.0).astype(BF16)
    carry = jnp.zeros((1, LANES), F32)
    qk_bound = _qk_bound(gq_ref[...], gk_ref[...])
    b_hi = qk_bound.astype(BF16).astype(F32)
    b_lo = (qk_bound - b_hi).astype(BF16).astype(F32)
    q_const = rows_ref[0:1, :]
    k_const = rows_ref[1:2, :] + b_hi * rows_ref[2:3, :] + b_lo * rows_ref[3:4, :]
    for i in range(s // blk):
        rows = slice(i * blk, (i + 1) * blk)
        lf = _log_sigmoid(small_ref[rows, :] + fb_ref[...]) * LOG2E
        hi, mid, lo = _split3(lf)
        c = _dot(tri, hi) + _dot(tri, mid) + _dot(tri, lo) + carry
        c_scr[rows, :] = c
        carry = c[blk - 1:blk, :]
        aug = _dot(jnp.concatenate(_split3(c), axis=1), p_ref[...])
        aq_ref[rows, :] = (aug[:, :LANES] + q_const).astype(BF16)
        ak_ref[rows, :] = (aug[:, LANES:] + k_const).astype(BF16)
    crow_ref[...] = c_scr[...].T[FF_ROW0:FF_ROW0 + SUBLANES, :]


def _qk_bound(gq, gk):
    gmax = jnp.max(jnp.abs(gq), axis=-1, keepdims=True) * jnp.max(jnp.abs(gk), axis=-1, keepdims=True)
    return gmax * (1.01 * LOG2E * HEAD_DIM ** 0.5)


def _fox_aug_constants():
    p = np.zeros((3 * LANES, 2 * LANES), np.float32)
    rows = np.zeros((SUBLANES, LANES), np.float32)
    for h in range(FOX_HEADS):
        base = FOX_AUG * h
        for j in range(3):
            p[j * LANES + SMALL_FF + h, base + j] = 1.0
            p[j * LANES + SMALL_FF + h, LANES + base + 3 + j] = -1.0
            rows[1, base + j] = 1.0
        rows[0, base + 3:base + 8] = 1.0
        rows[2, base + 6] = -1.0
        rows[3, base + 7] = -1.0
    return jnp.asarray(p, BF16), jnp.asarray(rows)


def _fox_gate(small, fb, gq, gk, batch, seq):
    t = small.shape[0]
    p, const_rows = _fox_aug_constants()
    fixed = lambda b: (0, 0)
    return pl.pallas_call(
        _foxgate_kernel,
        grid=(batch,),
        in_specs=[
            pl.BlockSpec((seq, LANES), lambda b: (b, 0)),
            pl.BlockSpec((1, LANES), fixed),
            pl.BlockSpec((1, HEAD_DIM), fixed),
            pl.BlockSpec((1, HEAD_DIM), fixed),
            pl.BlockSpec(p.shape, fixed),
            pl.BlockSpec(const_rows.shape, fixed),
        ],
        out_specs=[
            pl.BlockSpec((None, SUBLANES, seq), lambda b: (b, 0, 0)),
            pl.BlockSpec((seq, LANES), lambda b: (b, 0)),
            pl.BlockSpec((seq, LANES), lambda b: (b, 0)),
        ],
        out_shape=[
            jax.ShapeDtypeStruct((batch, SUBLANES, seq), F32),
            jax.ShapeDtypeStruct((t, LANES), BF16),
            jax.ShapeDtypeStruct((t, LANES), BF16),
        ],
        scratch_shapes=[pltpu.VMEM((seq, LANES), F32)],
        compiler_params=_params(("parallel",)),
        name="fox_gate",
    )(small, fb, gq, gk, p, const_rows)


def _fox_kernel(q_ref, k_ref, v_ref, fg_ref, aq_ref, ak_ref, crow_ref, gq_ref, gk_ref, ong_ref,
                o_ref, ka_scr, v1_scr):
    s = q_ref.shape[0]
    h = pl.program_id(1)
    scale = HEAD_DIM ** -0.5 * LOG2E
    lane = lax.broadcasted_iota(jnp.int32, (FOX_TQ, LANES), 1)
    own = (lane >= FOX_AUG * h) & (lane < FOX_AUG * (h + 1))
    qi = lax.broadcasted_iota(jnp.int32, (FOX_TQ, FOX_TQ), 0)
    ki = lax.broadcasted_iota(jnp.int32, (FOX_TQ, FOX_TQ), 1)
    causal = qi >= ki
    ones = jnp.ones((FOX_TQ, HEAD_DIM), BF16)

    def prepare(rows):
        qn = (_rms(q_ref[rows, :].astype(F32), gq_ref[...]) * scale).astype(BF16)
        kn = _rms(k_ref[rows, :].astype(F32), gk_ref[...]).astype(BF16)
        qa = jnp.concatenate(
            [qn, jnp.where(own, aq_ref[rows, :].astype(F32), 0.0).astype(BF16)], axis=1)
        ka = jnp.concatenate([kn, ak_ref[rows, :]], axis=1)
        v1 = jnp.concatenate([v_ref[rows, :], ones], axis=1)
        ka_scr[rows, :] = ka
        v1_scr[rows, :] = v1
        return qa, ka, v1

    def finish(rows, ol):
        o = ol[:, :HEAD_DIM] / ol[:, HEAD_DIM:]
        on = _rms(o, ong_ref[...])
        o_ref[rows, :] = (on * _sigmoid(fg_ref[rows, :].astype(F32))).astype(BF16)

    bounded = 2.0 * _qk_bound(gq_ref[...], gk_ref[...])[0, 0] <= FOX_MAX_SPAN

    @pl.when(bounded)
    def _():
        def logits(qb):
            r0 = qb * FOX_TQ
            rows = slice(r0, r0 + FOX_TQ)
            qa, ka, v1 = prepare(rows)
            diag = jnp.where(causal, _dot_nt(qa, ka), NEG_BIG)
            past = _dot_nt(qa, ka_scr[:r0, :]) if r0 else None
            return rows, r0, diag, past, v1

        def weighted_values(rows, r0, diag, past, v1):
            ol = _dot(jnp.exp2(diag).astype(BF16), v1)
            if r0:
                ol = ol + _dot(jnp.exp2(past).astype(BF16), v1_scr[:r0, :])
            finish(rows, ol)

        n_blocks = s // FOX_TQ
        pending = logits(0)
        for qb in range(n_blocks):
            nxt = logits(qb + 1) if qb + 1 < n_blocks else None
            weighted_values(*pending)
            pending = nxt

    @pl.when(jnp.logical_not(bounded))
    def _():
        for qb in range(s // FOX_TQ):
            r0 = qb * FOX_TQ
            rows = slice(r0, r0 + FOX_TQ)
            qa, ka, v1 = prepare(rows)
            qn = qa[:, :HEAD_DIM]
            diag = _dot_nt(qn, ka[:, :HEAD_DIM]) - crow_ref[:, rows]
            diag = jnp.where(causal, diag, NEG_BIG)
            m = jnp.max(diag, axis=-1, keepdims=True)
            if r0:
                past = _dot_nt(qn, ka_scr[:r0, :HEAD_DIM]) - crow_ref[:, :r0]
                m = jnp.maximum(m, jnp.max(past, axis=-1, keepdims=True))
                ol = _dot(jnp.exp2(past - m).astype(BF16), v1_scr[:r0, :])
                ol = ol + _dot(jnp.exp2(diag - m).astype(BF16), v1)
            else:
                ol = _dot(jnp.exp2(diag - m).astype(BF16), v1)
            finish(rows, ol)


def _fox(main, aq, ak, crow4, gq, gk, ong, batch, seq):
    t = main.shape[0]
    cb = lambda off: off // HEAD_DIM
    return pl.pallas_call(
        _fox_kernel,
        grid=(batch, FOX_HEADS),
        in_specs=[
            pl.BlockSpec((seq, HEAD_DIM), lambda b, h: (b, cb(OFF_FQ) + h)),
            pl.BlockSpec((seq, HEAD_DIM), lambda b, h: (b, cb(OFF_FK) + h)),
            pl.BlockSpec((seq, HEAD_DIM), lambda b, h: (b, cb(OFF_FV) + h)),
            pl.BlockSpec((seq, HEAD_DIM), lambda b, h: (b, cb(OFF_FG) + h)),
            pl.BlockSpec((seq, LANES), lambda b, h: (b, 0)),
            pl.BlockSpec((seq, LANES), lambda b, h: (b, 0)),
            pl.BlockSpec((None, None, 1, seq), lambda b, h: (b, h, 0, 0)),
            pl.BlockSpec((1, HEAD_DIM), lambda b, h: (0, 0)),
            pl.BlockSpec((1, HEAD_DIM), lambda b, h: (0, 0)),
            pl.BlockSpec((None, 1, HEAD_DIM), lambda b, h: (h, 0, 0)),
        ],
        out_specs=pl.BlockSpec((seq, HEAD_DIM), lambda b, h: (b, h)),
        out_shape=jax.ShapeDtypeStruct((t, FOX_HEADS * HEAD_DIM), BF16),
        scratch_shapes=[
            pltpu.VMEM((seq, HEAD_DIM + LANES), BF16),
            pltpu.VMEM((seq, 2 * HEAD_DIM), BF16),
        ],
        compiler_params=_params(("parallel", "arbitrary")),
        name="fox",
    )(main, main, main, main, aq, ak, crow4, gq, gk, ong)


def _memkv_kernel(mem_ref, g_ref, w_ref, gq_ref, gk_ref, mk_ref, mv_ref):
    mn = _rms(mem_ref[...], g_ref[...]).astype(BF16)
    kv = _dot(mn, w_ref[...])
    n_mem = kv.shape[0]
    dk = MEM_HEADS * HEAD_DIM
    ones = jnp.ones((n_mem, HEAD_DIM), BF16)
    qk_bound = _qk_bound(gq_ref[...], gk_ref[...])
    b_hi = qk_bound.astype(BF16).astype(F32)
    b_lo = (qk_bound - b_hi).astype(BF16).astype(F32)
    lane = lax.broadcasted_iota(jnp.int32, (n_mem, LANES), 1)
    shift = jnp.where(lane == 0, -b_hi, jnp.where(lane == 1, -b_lo, 0.0)).astype(BF16)
    for hh in range(MEM_HEADS):
        hl = slice(hh * HEAD_DIM, (hh + 1) * HEAD_DIM)
        mk_ref[:, 2 * hh * HEAD_DIM:(2 * hh + 1) * HEAD_DIM] = _rms(kv[:, hl], gk_ref[...]).astype(BF16)
        mk_ref[:, (2 * hh + 1) * HEAD_DIM:(2 * hh + 2) * HEAD_DIM] = shift
        mv_ref[:, 2 * hh * HEAD_DIM:(2 * hh + 1) * HEAD_DIM] = kv[:, dk + hh * HEAD_DIM:dk + (hh + 1) * HEAD_DIM].astype(BF16)
        mv_ref[:, (2 * hh + 1) * HEAD_DIM:(2 * hh + 2) * HEAD_DIM] = ones


def _mem_kv(mem2, g, w, gq, gk, batch, n_mem):
    d = mem2.shape[1]
    dk = MEM_HEADS * HEAD_DIM
    return pl.pallas_call(
        _memkv_kernel,
        grid=(batch,),
        in_specs=[
            pl.BlockSpec((n_mem, d), lambda b: (b, 0)),
            pl.BlockSpec((1, d), lambda b: (0, 0)),
            pl.BlockSpec((d, 2 * dk), lambda b: (0, 0)),
            pl.BlockSpec((1, HEAD_DIM), lambda b: (0, 0)),
            pl.BlockSpec((1, HEAD_DIM), lambda b: (0, 0)),
        ],
        out_specs=[
            pl.BlockSpec((n_mem, 2 * dk), lambda b: (b, 0)),
            pl.BlockSpec((n_mem, 2 * dk), lambda b: (b, 0)),
        ],
        out_shape=[
            jax.ShapeDtypeStruct((batch * n_mem, 2 * dk), BF16),
            jax.ShapeDtypeStruct((batch * n_mem, 2 * dk), BF16),
        ],
        compiler_params=_params(("parallel",)),
        name="mem_kv",
    )(mem2, g, w, gq, gk)


def _memattn_kernel(q_ref, mg_ref, mk_ref, mv_ref, gq_ref, gk_ref, ong_ref, o_ref):
    s = q_ref.shape[0]
    scale = HEAD_DIM ** -0.5 * LOG2E
    lane = lax.broadcasted_iota(jnp.int32, (MEM_TQ, LANES), 1)
    q_extra = jnp.where(lane < 2, 1.0, 0.0).astype(BF16)

    def q_block(rows):
        return (_rms(q_ref[rows, :].astype(F32), gq_ref[...]) * scale).astype(BF16)

    def finish(rows, ol):
        o = ol[:, :HEAD_DIM] / ol[:, HEAD_DIM:]
        on = _rms(o, ong_ref[...])
        o_ref[rows, :] = (on * _sigmoid(mg_ref[rows, :].astype(F32))).astype(BF16)

    bounded = 2.0 * _qk_bound(gq_ref[...], gk_ref[...])[0, 0] <= FOX_MAX_SPAN

    @pl.when(bounded)
    def _():
        def logits(qb):
            rows = slice(qb * MEM_TQ, (qb + 1) * MEM_TQ)
            qa = jnp.concatenate([q_block(rows), q_extra], axis=1)
            return rows, _dot_nt(qa, mk_ref[...])

        n_blocks = s // MEM_TQ
        pending = logits(0)
        for qb in range(n_blocks):
            nxt = logits(qb + 1) if qb + 1 < n_blocks else None
            rows, lg = pending
            finish(rows, _dot(jnp.exp2(lg).astype(BF16), mv_ref[...]))
            pending = nxt

    @pl.when(jnp.logical_not(bounded))
    def _():
        for qb in range(s // MEM_TQ):
            rows = slice(qb * MEM_TQ, (qb + 1) * MEM_TQ)
            logits = _dot_nt(q_block(rows), mk_ref[:, :HEAD_DIM])
            m = jnp.max(logits, axis=-1, keepdims=True)
            finish(rows, _dot(jnp.exp2(logits - m).astype(BF16), mv_ref[...]))


def _mem_attn(main, mkn, mv, gq, gk, ong, batch, seq, n_mem):
    t = main.shape[0]
    cb = lambda off: off // HEAD_DIM
    return pl.pallas_call(
        _memattn_kernel,
        grid=(batch, MEM_HEADS),
        in_specs=[
            pl.BlockSpec((seq, HEAD_DIM), lambda b, h: (b, cb(OFF_MQ) + h)),
            pl.BlockSpec((seq, HEAD_DIM), lambda b, h: (b, cb(OFF_MG) + h)),
            pl.BlockSpec((n_mem, 2 * HEAD_DIM), lambda b, h: (b, h)),
            pl.BlockSpec((n_mem, 2 * HEAD_DIM), lambda b, h: (b, h)),
            pl.BlockSpec((1, HEAD_DIM), lambda b, h: (0, 0)),
            pl.BlockSpec((1, HEAD_DIM), lambda b, h: (0, 0)),
            pl.BlockSpec((None, 1, HEAD_DIM), lambda b, h: (h, 0, 0)),
        ],
        out_specs=pl.BlockSpec((seq, HEAD_DIM), lambda b, h: (b, h)),
        out_shape=jax.ShapeDtypeStruct((t, MEM_HEADS * HEAD_DIM), BF16),
        compiler_params=_params(("parallel", "arbitrary")),
        name="mem_attn",
    )(main, main, mkn, mv, gq, gk, ong)


def _outproj_kernel(og_ref, of_ref, om_ref, x_ref, wo_ref, gm_ref, h_ref, hn_ref):
    n_g = og_ref.shape[1]
    n_f = of_ref.shape[1]
    d = x_ref.shape[1]
    slab = d // OUT_PROJ_SLABS
    sumsq = jnp.zeros((x_ref.shape[0], 1), F32)
    for c in range(OUT_PROJ_SLABS):
        cols = slice(c * slab, (c + 1) * slab)
        acc = _dot(og_ref[...], wo_ref[:n_g, cols])
        acc = acc + _dot(of_ref[...], wo_ref[n_g:n_g + n_f, cols])
        acc = acc + _dot(om_ref[...], wo_ref[n_g + n_f:, cols])
        h = x_ref[:, cols] + acc
        h_ref[:, cols] = h
        sumsq = sumsq + jnp.sum(h * h, axis=-1, keepdims=True)
    inv = lax.rsqrt(sumsq * (1.0 / d) + EPS)
    hn_ref[...] = (h_ref[...] * inv * gm_ref[...]).astype(BF16)


def _out_proj(og, of, om, x2, wo, gm, tm=512):
    t, d = x2.shape
    row = lambda i: (i, 0)
    fixed = lambda i: (0, 0)
    return pl.pallas_call(
        _outproj_kernel,
        grid=(t // tm,),
        in_specs=[
            pl.BlockSpec((tm, og.shape[1]), row),
            pl.BlockSpec((tm, of.shape[1]), row),
            pl.BlockSpec((tm, om.shape[1]), row),
            pl.BlockSpec((tm, d), row),
            pl.BlockSpec(wo.shape, fixed),
            pl.BlockSpec((1, d), fixed),
        ],
        out_specs=[pl.BlockSpec((tm, d), row), pl.BlockSpec((tm, d), row)],
        out_shape=[jax.ShapeDtypeStruct((t, d), F32), jax.ShapeDtypeStruct((t, d), BF16)],
        compiler_params=_params(("parallel",)),
        name="out_proj",
    )(og, of, om, x2, wo, gm)


def _mlp_kernel(hn_ref, h_ref, wu_ref, wd_ref, o_ref):
    @pl.when(pl.program_id(1) == 0)
    def _():
        o_ref[...] = h_ref[...]

    tf = wu_ref.shape[1]
    piece = tf // MLP_SPLIT
    acc = None
    for s in range(MLP_SPLIT):
        cols = slice(s * piece, (s + 1) * piece)
        u = jnp.maximum(_dot(hn_ref[...], wu_ref[:, cols]), 0.0)
        part = _dot((u * u).astype(BF16), wd_ref[cols, :])
        acc = part if acc is None else acc + part
    o_ref[...] += acc


def _mlp(hn, h, wu, wd, tm=512, tf=2048):
    t, d = h.shape
    dff = wu.shape[1]
    blocks = 2 * (2 * d * tf * 2 + tm * d * 2 + 2 * tm * d * 4)
    vmem_limit = blocks + MLP_VMEM_TEMPS
    return pl.pallas_call(
        _mlp_kernel,
        grid=(t // tm, dff // tf),
        in_specs=[
            pl.BlockSpec((tm, d), lambda i, f: (i, 0)),
            pl.BlockSpec((tm, d), lambda i, f: (i, 0)),
            pl.BlockSpec((d, tf), lambda i, f: (0, f)),
            pl.BlockSpec((tf, d), lambda i, f: (f, 0)),
        ],
        out_specs=pl.BlockSpec((tm, d), lambda i, f: (i, 0)),
        out_shape=jax.ShapeDtypeStruct((t, d), F32),
        compiler_params=_params(("parallel", "arbitrary"), vmem_limit),
        name="mlp",
    )(hn, h, wu, wd)


def _regroup_kernel(w_ref, main_ref, small_ref):
    n_gla = OFF_FQ
    n_fox = OFF_MQ - OFF_FQ
    a0 = n_gla
    f0 = a0 + GLA_RANK
    ff0 = f0 + n_fox
    m0 = ff0 + FOX_HEADS
    n_gq = GLA_HEADS * GLA_DK
    q_scale = GLA_DK ** -0.5
    assert math.frexp(q_scale)[0] == 0.5
    main_ref[:, :n_gq] = (w_ref[:, :n_gq] * q_scale).astype(BF16)
    main_ref[:, n_gq:n_gla] = w_ref[:, n_gq:n_gla].astype(BF16)
    main_ref[:, OFF_FQ:OFF_MQ] = w_ref[:, f0:ff0].astype(BF16)
    main_ref[:, OFF_MQ:] = w_ref[:, m0:].astype(BF16)
    rows = w_ref.shape[0]
    pad = jnp.zeros((rows, LANES - GLA_RANK - FOX_HEADS), F32)
    small_ref[...] = jnp.concatenate([w_ref[:, a0:f0], w_ref[:, ff0:m0], pad], axis=1).astype(BF16)


def _regroup_w_in(w_in, tr=256):
    d, n_in = w_in.shape
    return pl.pallas_call(
        _regroup_kernel,
        grid=(d // tr,),
        in_specs=[pl.BlockSpec((tr, n_in), lambda i: (i, 0))],
        out_specs=[pl.BlockSpec((tr, N_MAIN), lambda i: (i, 0)),
                   pl.BlockSpec((tr, LANES), lambda i: (i, 0))],
        out_shape=[jax.ShapeDtypeStruct((d, N_MAIN), BF16), jax.ShapeDtypeStruct((d, LANES), BF16)],
        compiler_params=_params(("parallel",)),
        name="regroup_w_in",
    )(w_in)


def kernel(x, mem, attn_norm_g, w_in, gla_a_w2, gla_a_b, fox_f_b, fox_q_norm_g, fox_k_norm_g,
           mem_norm_g, w_mem_kv, mem_q_norm_g, mem_k_norm_g, out_norm_g, w_out, mlp_norm_g,
           w_up, w_down):
    batch, seq, d = x.shape
    n_mem = mem.shape[1]
    t = batch * seq
    x2 = x.reshape(t, d)
    row = lambda a: a.reshape(1, -1).astype(F32)

    w_main, w_small = _regroup_w_in(w_in)
    main, small = _in_proj(x2, row(attn_norm_g), w_main, w_small)

    n_gla = GLA_HEADS * HEAD_DIM
    n_fox = FOX_HEADS * HEAD_DIM
    fb = jnp.zeros((1, LANES), F32).at[0, SMALL_FF:SMALL_FF + FOX_HEADS].set(fox_f_b)
    crow, aq, ak = _fox_gate(small, fb, row(fox_q_norm_g), row(fox_k_norm_g), batch, seq)
    ong_fox = out_norm_g[n_gla:n_gla + n_fox].reshape(FOX_HEADS, 1, HEAD_DIM)
    of = _fox(main, aq, ak, crow.reshape(batch, SUBLANES, 1, seq), row(fox_q_norm_g),
              row(fox_k_norm_g), ong_fox, batch, seq)

    mkn, mv = _mem_kv(mem.reshape(batch * n_mem, d), row(mem_norm_g), w_mem_kv.astype(BF16),
                      row(mem_q_norm_g), row(mem_k_norm_g), batch, n_mem)
    ong_mem = out_norm_g[n_gla + n_fox:].reshape(MEM_HEADS, 1, HEAD_DIM)
    om = _mem_attn(main, mkn, mv, row(mem_q_norm_g), row(mem_k_norm_g), ong_mem, batch, seq, n_mem)

    w2p = jnp.zeros((LANES, GLA_HEADS * GLA_DK), F32).at[:GLA_RANK].set(gla_a_w2).astype(BF16)
    og = _gla(main, small, w2p, row(gla_a_b), row(out_norm_g[:n_gla]), batch, seq)

    h, hn = _out_proj(og, of, om, x2, w_out.astype(BF16), row(mlp_norm_g))
    y = _mlp(hn, h, w_up.astype(BF16), w_down.astype(BF16))
    return y.reshape(batch, seq, d)
```

```python
import functools
import math

import jax
import jax.numpy as jnp
import numpy as np
from jax import lax
from jax.experimental import pallas as pl
from jax.experimental.pallas import tpu as pltpu

F32 = jnp.float32
BF16 = jnp.bfloat16

HEAD_DIM = 128
GLA_HEADS = 8
GLA_DK = 64
GLA_RANK = 16
GLA_TAU = 16.0
FOX_HEADS = 4
MEM_HEADS = 4
EPS = 1e-6

LANES = 128
SUBLANES = 8
LOG2E = 1.4426950408889634
GLA_CHUNK = 64
GLA_STEP = 512
GLA_CUMSUM_BLOCK = 256
GLA_SAFE_EXP = 86.0
FOX_TQ = 256
FOX_AUG = 8
FOX_MAX_SPAN = 100.0
MEM_TQ = 512
OUT_PROJ_SLABS = 4
NEG_BIG = -1e30
VMEM_LIMIT = 50 * 1024 * 1024
MLP_VMEM_TEMPS = 5 * 1024 * 1024
MLP_SPLIT = 2

OFF_GQ, OFF_GK, OFF_GV, OFF_GG = 0, 512, 1024, 2048
OFF_FQ, OFF_FK, OFF_FV, OFF_FG = 3072, 3584, 4096, 4608
OFF_MQ, OFF_MG = 5120, 5632
N_MAIN = 6144
SMALL_FF = GLA_RANK
FF_ROW0 = SMALL_FF
assert FF_ROW0 % SUBLANES == 0 and FOX_HEADS <= SUBLANES


def _dot(a, b):
    return jnp.dot(a, b, preferred_element_type=F32)


def _dot_nt(a, b):
    return lax.dot_general(a, b, (((1,), (1,)), ((), ())), preferred_element_type=F32)


def _dot_tn(a, b):
    return lax.dot_general(a, b, (((0,), (0,)), ((), ())), preferred_element_type=F32)


def _rms(x, g):
    return x * lax.rsqrt(jnp.mean(x * x, axis=-1, keepdims=True) + EPS) * g


def _log_sigmoid(z):
    return jnp.minimum(z, 0.0) - jnp.log(1.0 + jnp.exp(-jnp.abs(z)))


def _sigmoid(z):
    return 1.0 / (1.0 + jnp.exp(-z))


def _split3(a):
    hi = a.astype(BF16)
    r = a - hi.astype(F32)
    mid = r.astype(BF16)
    lo = (r - mid.astype(F32)).astype(BF16)
    return hi, mid, lo


def _params(sem, vmem_limit=VMEM_LIMIT):
    return pltpu.CompilerParams(dimension_semantics=sem, vmem_limit_bytes=vmem_limit)


def _inproj_kernel(x_ref, g_ref, wm_ref, ws_ref, main_ref, small_ref, xn_even, xn_odd):
    i = pl.program_id(0)
    j = pl.program_id(1)
    n_col = pl.num_programs(1)
    rs = x_ref.shape[0] // n_col
    odd = i % 2 == 1

    def normalise_slice(dst):
        rows = pl.ds(pl.multiple_of(j * rs, rs), rs)
        dst[rows, :] = _rms(x_ref[rows, :], g_ref[...]).astype(BF16)

    def step(src, dst):
        @pl.when(j == 0)
        def _():
            small_ref[...] = _dot_nt(src[...], ws_ref[...])

        normalise_slice(dst)
        main_ref[...] = _dot_nt(src[...], wm_ref[...]).astype(BF16)

    @pl.when(i == 0)
    def _():
        normalise_slice(xn_even)

    @pl.when(odd)
    def _():
        step(xn_even, xn_odd)

    @pl.when((i > 0) & jnp.logical_not(odd))
    def _():
        step(xn_odd, xn_even)


def _in_proj(x2, g, w_main, w_small, tm=1024, tn=1536):
    t, d = x2.shape
    n_row = t // tm
    prev = lambda i: jnp.maximum(i - 1, 0)
    return pl.pallas_call(
        _inproj_kernel,
        grid=(n_row + 1, N_MAIN // tn),
        in_specs=[
            pl.BlockSpec((tm, d), lambda i, j: (jnp.minimum(i, n_row - 1), 0)),
            pl.BlockSpec((1, d), lambda i, j: (0, 0)),
            pl.BlockSpec((tn, d), lambda i, j: (j, 0)),
            pl.BlockSpec((LANES, d), lambda i, j: (0, 0)),
        ],
        out_specs=[
            pl.BlockSpec((tm, tn), lambda i, j: (prev(i), jnp.where(i > 0, j, 0))),
            pl.BlockSpec((tm, LANES), lambda i, j: (prev(i), 0)),
        ],
        out_shape=[
            jax.ShapeDtypeStruct((t, N_MAIN), BF16),
            jax.ShapeDtypeStruct((t, LANES), F32),
        ],
        scratch_shapes=[pltpu.VMEM((tm, d), BF16), pltpu.VMEM((tm, d), BF16)],
        compiler_params=_params(("arbitrary", "arbitrary")),
        name="in_proj",
    )(x2, g, w_main, w_small)


def _gla_kernel(q_ref, k_ref, v_ref, gg_ref, small_ref, small_next_ref, w2_ref, ab_ref, ong_ref,
                o_ref, st_ref, bc_scr, safe_ref, qs_scr, bs_scr, os_scr):
    C = GLA_CHUNK
    n_pairs = GLA_HEADS // 2
    step = pl.program_id(0) * pl.num_programs(1) + pl.program_id(1)
    slot = step % 2

    @pl.when(pl.program_id(1) == 0)
    def _():
        st_ref[...] = jnp.zeros_like(st_ref)

    def decay_sums(src_ref, dst):
        z = _dot(src_ref[...].astype(BF16), w2_ref[...]) + ab_ref[...]
        la = _log_sigmoid(z) * (LOG2E / GLA_TAU)
        blk = GLA_CUMSUM_BLOCK
        ri = lax.broadcasted_iota(jnp.int32, (blk, blk), 0)
        ci = lax.broadcasted_iota(jnp.int32, (blk, blk), 1)
        same_chunk = (ri ^ ci) < C
        tri = jnp.where(same_chunk, jnp.where(ci <= ri, 1.0, 0.0), 0.0).astype(BF16)
        for i in range(GLA_STEP // blk):
            rows = slice(i * blk, (i + 1) * blk)
            hi, mid, lo = _split3(la[rows, :])
            bc_scr[dst, rows, :] = _dot(tri, hi) + _dot(tri, mid) + _dot(tri, lo)
        safe_ref[dst] = (jnp.min(la) * C >= -GLA_SAFE_EXP).astype(jnp.int32)

    @pl.when(step == 0)
    def _():
        decay_sums(small_ref, 0)

    lane = lax.broadcasted_iota(jnp.int32, (C, LANES), 1)
    low_half = lane < GLA_DK
    st_r = lax.broadcasted_iota(jnp.int32, (2 * HEAD_DIM, LANES), 0)
    st_c = lax.broadcasted_iota(jnp.int32, (2 * HEAD_DIM, LANES), 1)
    st_mask = (st_r < HEAD_DIM) == (st_c < GLA_DK)
    a_r = lax.broadcasted_iota(jnp.int32, (C, 2 * C), 0)
    a_c = lax.broadcasted_iota(jnp.int32, (C, 2 * C), 1)
    causal = a_r >= (a_c & (C - 1))
    zeros_v = jnp.zeros((C, HEAD_DIM), BF16)
    jrow = lax.broadcasted_iota(jnp.int32, (C, LANES), 0)

    def unit_first_half(ch, p, fast):
        rows = slice(ch * C, (ch + 1) * C)
        kl = slice(p * LANES, (p + 1) * LANES)
        vl = slice(p * 2 * HEAD_DIM, (p + 1) * 2 * HEAD_DIM)
        b = bc_scr[slot, rows, kl]
        qs = q_ref[rows, kl].astype(F32)
        kf = k_ref[rows, kl].astype(F32)
        v2 = v_ref[rows, vl]
        qe16 = (qs * jnp.exp2(b)).astype(BF16)
        b_last = b[C - 1:C, :]
        st = st_ref[p]
        o = _dot_nt(qe16, st.astype(BF16))
        if fast:
            ke = kf * jnp.exp2(-b)
            kbd = jnp.concatenate(
                [jnp.where(low_half, ke, 0.0), jnp.where(low_half, 0.0, ke)], axis=0).astype(BF16)
            scores = _dot_nt(qe16, kbd)
            vbd = jnp.concatenate(
                [jnp.concatenate([v2[:, :HEAD_DIM], zeros_v], axis=1),
                 jnp.concatenate([zeros_v, v2[:, HEAD_DIM:]], axis=1)], axis=0)
            upd = _dot_tn(vbd, kbd)
            st_ref[p] = jnp.exp2(b_last) * (st + upd)
            return rows, p, o, scores, vbd
        else:
            kd16 = (kf * jnp.exp2(b_last - b)).astype(BF16)
            upd = _dot_tn(v2, kd16)
            st_ref[p] = jnp.exp2(b_last) * st + jnp.where(st_mask, upd, 0.0)
            qs_scr[...] = qs
            bs_scr[...] = b
            v2f = v2.astype(F32)

            def row(i, carry):
                qi = qs_scr[pl.ds(i, 1), :]
                bi = bs_scr[pl.ds(i, 1), :]
                w = qi * kf * jnp.exp2(jnp.minimum(bi - b, 0.0))
                w = jnp.where(jrow <= i, w, 0.0)
                p0 = jnp.sum(jnp.where(low_half, w, 0.0), axis=1, keepdims=True)
                p1 = jnp.sum(jnp.where(low_half, 0.0, w), axis=1, keepdims=True)
                pv = jnp.concatenate([p0 * v2f[:, :HEAD_DIM], p1 * v2f[:, HEAD_DIM:]], axis=1)
                os_scr[pl.ds(i, 1), :] = jnp.sum(pv, axis=0, keepdims=True)
                return carry

            lax.fori_loop(0, C, row, 0)
            return rows, p, o + os_scr[...], None, None

    def unit_second_half(rows, p, o, scores, vbd):
        if scores is not None:
            o = o + _dot(jnp.where(causal, scores, 0.0).astype(BF16), vbd)
        for hh in range(2):
            hl = slice((2 * p + hh) * HEAD_DIM, (2 * p + hh + 1) * HEAD_DIM)
            on = _rms(o[:, hh * HEAD_DIM:(hh + 1) * HEAD_DIM], ong_ref[:, hl])
            g = gg_ref[rows, hl].astype(F32)
            o_ref[rows, hl] = (on * (g * _sigmoid(g))).astype(BF16)

    def run(fast):
        n_chunks = GLA_STEP // C
        for ch in range(n_chunks):
            if ch == n_chunks // 2:
                decay_sums(small_next_ref, 1 - slot)
            for p in range(n_pairs):
                unit_second_half(*unit_first_half(ch, p, fast))

    safe = safe_ref[slot]

    @pl.when(safe == 1)
    def _():
        run(True)

    @pl.when(safe == 0)
    def _():
        run(False)


def _gla(main, small, w2p, ab, ong, batch, seq):
    t = main.shape[0]
    nc = seq // GLA_STEP
    rb = lambda b, c: b * nc + c
    rb_next = lambda b, c: jnp.minimum(b * nc + c + 1, batch * nc - 1)
    dq = GLA_HEADS * GLA_DK
    dv = GLA_HEADS * HEAD_DIM
    return pl.pallas_call(
        _gla_kernel,
        grid=(batch, nc),
        in_specs=[
            pl.BlockSpec((GLA_STEP, dq), lambda b, c: (rb(b, c), OFF_GQ // dq)),
            pl.BlockSpec((GLA_STEP, dq), lambda b, c: (rb(b, c), OFF_GK // dq)),
            pl.BlockSpec((GLA_STEP, dv), lambda b, c: (rb(b, c), OFF_GV // dv)),
            pl.BlockSpec((GLA_STEP, dv), lambda b, c: (rb(b, c), OFF_GG // dv)),
            pl.BlockSpec((GLA_STEP, LANES), lambda b, c: (rb(b, c), 0)),
            pl.BlockSpec((GLA_STEP, LANES), lambda b, c: (rb_next(b, c), 0)),
            pl.BlockSpec((LANES, dq), lambda b, c: (0, 0)),
            pl.BlockSpec((1, dq), lambda b, c: (0, 0)),
            pl.BlockSpec((1, dv), lambda b, c: (0, 0)),
        ],
        out_specs=pl.BlockSpec((GLA_STEP, dv), lambda b, c: (rb(b, c), 0)),
        out_shape=jax.ShapeDtypeStruct((t, dv), BF16),
        scratch_shapes=[
            pltpu.VMEM((GLA_HEADS // 2, 2 * HEAD_DIM, LANES), F32),
            pltpu.VMEM((2, GLA_STEP, dq), F32),
            pltpu.SMEM((2,), jnp.int32),
            pltpu.VMEM((GLA_CHUNK, LANES), F32),
            pltpu.VMEM((GLA_CHUNK, LANES), F32),
            pltpu.VMEM((GLA_CHUNK, 2 * HEAD_DIM), F32),
        ],
        compiler_params=_params(("arbitrary", "arbitrary")),
        name="gla",
    )(main, main, main, main, small, small, w2p, ab, ong)


def _foxgate_kernel(small_ref, fb_ref, gq_ref, gk_ref, p_ref, rows_ref,
                    crow_ref, aq_ref, ak_ref):
    s = small_ref.shape[0]
    blk = 256
    ff = small_ref[...].T[FF_ROW0:FF_ROW0 + SUBLANES, :]
    lf = _log_sigmoid(ff + fb_ref[...]) * LOG2E
    ri = lax.broadcasted_iota(jnp.int32, (blk, blk), 0)
    ci = lax.broadcasted_iota(jnp.int32, (blk, blk), 1)
    tri = jnp.where(ri <= ci, 1.0, 0.0).astype(BF16)
    carry = jnp.zeros((SUBLANES, 1), F32)
    for i in range(s // blk):
        cols = slice(i * blk, (i + 1) * blk)
        hi, mid, lo = _split3(lf[:, cols])
        c = _dot(hi, tri) + _dot(mid, tri) + _dot(lo, tri) + carry
        crow_ref[:, cols] = c
        carry = c[:, blk - 1:blk]
    qk_bound = _qk_bound(gq_ref[...], gk_ref[...])
    b_hi = qk_bound.astype(BF16).astype(F32)
    b_lo = (qk_bound - b_hi).astype(BF16).astype(F32)
    q_const = rows_ref[0:1, :]
    k_const = rows_ref[1:2, :] + b_hi * rows_ref[2:3, :] + b_lo * rows_ref[3:4, :]
    hi, mid, lo = _split3(crow_ref[...])
    terms = jnp.concatenate([hi.astype(F32), mid.astype(F32), lo.astype(F32),
                             jnp.zeros((SUBLANES, s), F32)], axis=0).astype(BF16)
    aug = _dot_tn(terms, p_ref[...])
    aq_ref[...] = (aug[:, :LANES] + q_const).astype(BF16)
    ak_ref[...] = (aug[:, LANES:] + k_const).astype(BF16)


def _qk_bound(gq, gk):
    gmax = jnp.max(jnp.abs(gq), axis=-1, keepdims=True) * jnp.max(jnp.abs(gk), axis=-1, keepdims=True)
    return gmax * (1.01 * LOG2E * HEAD_DIM ** 0.5)


def _fox_aug_constants():
    p = np.zeros((4 * SUBLANES, 2 * LANES), np.float32)
    rows = np.zeros((SUBLANES, LANES), np.float32)
    for h in range(FOX_HEADS):
        base = FOX_AUG * h
        for j in range(3):
            p[j * SUBLANES + h, base + j] = 1.0
            p[j * SUBLANES + h, LANES + base + 3 + j] = -1.0
            rows[1, base + j] = 1.0
        rows[0, base + 3:base + 8] = 1.0
        rows[2, base + 6] = -1.0
        rows[3, base + 7] = -1.0
    return jnp.asarray(p, BF16), jnp.asarray(rows)


def _fox_gate(small, fb, gq, gk, batch, seq):
    t = small.shape[0]
    p, const_rows = _fox_aug_constants()
    fixed = lambda b: (0, 0)
    return pl.pallas_call(
        _foxgate_kernel,
        grid=(batch,),
        in_specs=[
            pl.BlockSpec((seq, LANES), lambda b: (b, 0)),
            pl.BlockSpec((SUBLANES, 1), fixed),
            pl.BlockSpec((1, HEAD_DIM), fixed),
            pl.BlockSpec((1, HEAD_DIM), fixed),
            pl.BlockSpec(p.shape, fixed),
            pl.BlockSpec(const_rows.shape, fixed),
        ],
        out_specs=[
            pl.BlockSpec((None, SUBLANES, seq), lambda b: (b, 0, 0)),
            pl.BlockSpec((seq, LANES), lambda b: (b, 0)),
            pl.BlockSpec((seq, LANES), lambda b: (b, 0)),
        ],
        out_shape=[
            jax.ShapeDtypeStruct((batch, SUBLANES, seq), F32),
            jax.ShapeDtypeStruct((t, LANES), BF16),
            jax.ShapeDtypeStruct((t, LANES), BF16),
        ],
        compiler_params=_params(("parallel",)),
        name="fox_gate",
    )(small, fb, gq, gk, p, const_rows)


def _fox_kernel(q_ref, k_ref, v_ref, fg_ref, aq_ref, ak_ref, crow_ref, gq_ref, gk_ref, ong_ref,
                o_ref, ka_scr, v1_scr):
    s = q_ref.shape[0]
    h = pl.program_id(1)
    scale = HEAD_DIM ** -0.5 * LOG2E
    lane = lax.broadcasted_iota(jnp.int32, (FOX_TQ, LANES), 1)
    own = (lane >= FOX_AUG * h) & (lane < FOX_AUG * (h + 1))
    qi = lax.broadcasted_iota(jnp.int32, (FOX_TQ, FOX_TQ), 0)
    ki = lax.broadcasted_iota(jnp.int32, (FOX_TQ, FOX_TQ), 1)
    causal = qi >= ki
    ones = jnp.ones((FOX_TQ, HEAD_DIM), BF16)

    def prepare(rows):
        qn = (_rms(q_ref[rows, :].astype(F32), gq_ref[...]) * scale).astype(BF16)
        kn = _rms(k_ref[rows, :].astype(F32), gk_ref[...]).astype(BF16)
        qa = jnp.concatenate(
            [qn, jnp.where(own, aq_ref[rows, :].astype(F32), 0.0).astype(BF16)], axis=1)
        ka = jnp.concatenate([kn, ak_ref[rows, :]], axis=1)
        v1 = jnp.concatenate([v_ref[rows, :], ones], axis=1)
        ka_scr[rows, :] = ka
        v1_scr[rows, :] = v1
        return qa, ka, v1

    def finish(rows, ol):
        o = ol[:, :HEAD_DIM] / ol[:, HEAD_DIM:]
        on = _rms(o, ong_ref[...])
        o_ref[rows, :] = (on * _sigmoid(fg_ref[rows, :].astype(F32))).astype(BF16)

    bounded = 2.0 * _qk_bound(gq_ref[...], gk_ref[...])[0, 0] <= FOX_MAX_SPAN

    @pl.when(bounded)
    def _():
        def logits(qb):
            r0 = qb * FOX_TQ
            rows = slice(r0, r0 + FOX_TQ)
            qa, ka, v1 = prepare(rows)
            diag = jnp.where(causal, _dot_nt(qa, ka), NEG_BIG)
            past = _dot_nt(qa, ka_scr[:r0, :]) if r0 else None
            return rows, r0, diag, past, v1

        def weighted_values(rows, r0, diag, past, v1):
            ol = _dot(jnp.exp2(diag).astype(BF16), v1)
            if r0:
                ol = ol + _dot(jnp.exp2(past).astype(BF16), v1_scr[:r0, :])
            finish(rows, ol)

        n_blocks = s // FOX_TQ
        pending = logits(0)
        for qb in range(n_blocks):
            nxt = logits(qb + 1) if qb + 1 < n_blocks else None
            weighted_values(*pending)
            pending = nxt

    @pl.when(jnp.logical_not(bounded))
    def _():
        for qb in range(s // FOX_TQ):
            r0 = qb * FOX_TQ
            rows = slice(r0, r0 + FOX_TQ)
            qa, ka, v1 = prepare(rows)
            qn = qa[:, :HEAD_DIM]
            diag = _dot_nt(qn, ka[:, :HEAD_DIM]) - crow_ref[:, rows]
            diag = jnp.where(causal, diag, NEG_BIG)
            m = jnp.max(diag, axis=-1, keepdims=True)
            if r0:
                past = _dot_nt(qn, ka_scr[:r0, :HEAD_DIM]) - crow_ref[:, :r0]
                m = jnp.maximum(m, jnp.max(past, axis=-1, keepdims=True))
                ol = _dot(jnp.exp2(past - m).astype(BF16), v1_scr[:r0, :])
                ol = ol + _dot(jnp.exp2(diag - m).astype(BF16), v1)
            else:
                ol = _dot(jnp.exp2(diag - m).astype(BF16), v1)
            finish(rows, ol)


def _fox(main, aq, ak, crow4, gq, gk, ong, batch, seq):
    t = main.shape[0]
    cb = lambda off: off // HEAD_DIM
    return pl.pallas_call(
        _fox_kernel,
        grid=(batch, FOX_HEADS),
        in_specs=[
            pl.BlockSpec((seq, HEAD_DIM), lambda b, h: (b, cb(OFF_FQ) + h)),
            pl.BlockSpec((seq, HEAD_DIM), lambda b, h: (b, cb(OFF_FK) + h)),
            pl.BlockSpec((seq, HEAD_DIM), lambda b, h: (b, cb(OFF_FV) + h)),
            pl.BlockSpec((seq, HEAD_DIM), lambda b, h: (b, cb(OFF_FG) + h)),
            pl.BlockSpec((seq, LANES), lambda b, h: (b, 0)),
            pl.BlockSpec((seq, LANES), lambda b, h: (b, 0)),
            pl.BlockSpec((None, None, 1, seq), lambda b, h: (b, h, 0, 0)),
            pl.BlockSpec((1, HEAD_DIM), lambda b, h: (0, 0)),
            pl.BlockSpec((1, HEAD_DIM), lambda b, h: (0, 0)),
            pl.BlockSpec((None, 1, HEAD_DIM), lambda b, h: (h, 0, 0)),
        ],
        out_specs=pl.BlockSpec((seq, HEAD_DIM), lambda b, h: (b, h)),
        out_shape=jax.ShapeDtypeStruct((t, FOX_HEADS * HEAD_DIM), BF16),
        scratch_shapes=[
            pltpu.VMEM((seq, HEAD_DIM + LANES), BF16),
            pltpu.VMEM((seq, 2 * HEAD_DIM), BF16),
        ],
        compiler_params=_params(("parallel", "arbitrary")),
        name="fox",
    )(main, main, main, main, aq, ak, crow4, gq, gk, ong)


def _memkv_kernel(mem_ref, g_ref, w_ref, gq_ref, gk_ref, mk_ref, mv_ref):
    mn = _rms(mem_ref[...], g_ref[...]).astype(BF16)
    kv = _dot(mn, w_ref[...])
    n_mem = kv.shape[0]
    dk = MEM_HEADS * HEAD_DIM
    ones = jnp.ones((n_mem, HEAD_DIM), BF16)
    qk_bound = _qk_bound(gq_ref[...], gk_ref[...])
    b_hi = qk_bound.astype(BF16).astype(F32)
    b_lo = (qk_bound - b_hi).astype(BF16).astype(F32)
    lane = lax.broadcasted_iota(jnp.int32, (n_mem, LANES), 1)
    shift = jnp.where(lane == 0, -b_hi, jnp.where(lane == 1, -b_lo, 0.0)).astype(BF16)
    for hh in range(MEM_HEADS):
        hl = slice(hh * HEAD_DIM, (hh + 1) * HEAD_DIM)
        mk_ref[:, 2 * hh * HEAD_DIM:(2 * hh + 1) * HEAD_DIM] = _rms(kv[:, hl], gk_ref[...]).astype(BF16)
        mk_ref[:, (2 * hh + 1) * HEAD_DIM:(2 * hh + 2) * HEAD_DIM] = shift
        mv_ref[:, 2 * hh * HEAD_DIM:(2 * hh + 1) * HEAD_DIM] = kv[:, dk + hh * HEAD_DIM:dk + (hh + 1) * HEAD_DIM].astype(BF16)
        mv_ref[:, (2 * hh + 1) * HEAD_DIM:(2 * hh + 2) * HEAD_DIM] = ones


def _mem_kv(mem2, g, w, gq, gk, batch, n_mem):
    d = mem2.shape[1]
    dk = MEM_HEADS * HEAD_DIM
    return pl.pallas_call(
        _memkv_kernel,
        grid=(batch,),
        in_specs=[
            pl.BlockSpec((n_mem, d), lambda b: (b, 0)),
            pl.BlockSpec((1, d), lambda b: (0, 0)),
            pl.BlockSpec((d, 2 * dk), lambda b: (0, 0)),
            pl.BlockSpec((1, HEAD_DIM), lambda b: (0, 0)),
            pl.BlockSpec((1, HEAD_DIM), lambda b: (0, 0)),
        ],
        out_specs=[
            pl.BlockSpec((n_mem, 2 * dk), lambda b: (b, 0)),
            pl.BlockSpec((n_mem, 2 * dk), lambda b: (b, 0)),
        ],
        out_shape=[
            jax.ShapeDtypeStruct((batch * n_mem, 2 * dk), BF16),
            jax.ShapeDtypeStruct((batch * n_mem, 2 * dk), BF16),
        ],
        compiler_params=_params(("parallel",)),
        name="mem_kv",
    )(mem2, g, w, gq, gk)


def _memattn_kernel(q_ref, mg_ref, mk_ref, mv_ref, gq_ref, gk_ref, ong_ref, o_ref):
    s = q_ref.shape[0]
    scale = HEAD_DIM ** -0.5 * LOG2E
    lane = lax.broadcasted_iota(jnp.int32, (MEM_TQ, LANES), 1)
    q_extra = jnp.where(lane < 2, 1.0, 0.0).astype(BF16)

    def q_block(rows):
        return (_rms(q_ref[rows, :].astype(F32), gq_ref[...]) * scale).astype(BF16)

    def finish(rows, ol):
        o = ol[:, :HEAD_DIM] / ol[:, HEAD_DIM:]
        on = _rms(o, ong_ref[...])
        o_ref[rows, :] = (on * _sigmoid(mg_ref[rows, :].astype(F32))).astype(BF16)

    bounded = 2.0 * _qk_bound(gq_ref[...], gk_ref[...])[0, 0] <= FOX_MAX_SPAN

    @pl.when(bounded)
    def _():
        def logits(qb):
            rows = slice(qb * MEM_TQ, (qb + 1) * MEM_TQ)
            qa = jnp.concatenate([q_block(rows), q_extra], axis=1)
            return rows, _dot_nt(qa, mk_ref[...])

        n_blocks = s // MEM_TQ
        pending = logits(0)
        for qb in range(n_blocks):
            nxt = logits(qb + 1) if qb + 1 < n_blocks else None
            rows, lg = pending
            finish(rows, _dot(jnp.exp2(lg).astype(BF16), mv_ref[...]))
            pending = nxt

    @pl.when(jnp.logical_not(bounded))
    def _():
        for qb in range(s // MEM_TQ):
            rows = slice(qb * MEM_TQ, (qb + 1) * MEM_TQ)
            logits = _dot_nt(q_block(rows), mk_ref[:, :HEAD_DIM])
            m = jnp.max(logits, axis=-1, keepdims=True)
            finish(rows, _dot(jnp.exp2(logits - m).astype(BF16), mv_ref[...]))


def _mem_attn(main, mkn, mv, gq, gk, ong, batch, seq, n_mem):
    t = main.shape[0]
    cb = lambda off: off // HEAD_DIM
    return pl.pallas_call(
        _memattn_kernel,
        grid=(batch, MEM_HEADS),
        in_specs=[
            pl.BlockSpec((seq, HEAD_DIM), lambda b, h: (b, cb(OFF_MQ) + h)),
            pl.BlockSpec((seq, HEAD_DIM), lambda b, h: (b, cb(OFF_MG) + h)),
            pl.BlockSpec((n_mem, 2 * HEAD_DIM), lambda b, h: (b, h)),
            pl.BlockSpec((n_mem, 2 * HEAD_DIM), lambda b, h: (b, h)),
            pl.BlockSpec((1, HEAD_DIM), lambda b, h: (0, 0)),
            pl.BlockSpec((1, HEAD_DIM), lambda b, h: (0, 0)),
            pl.BlockSpec((None, 1, HEAD_DIM), lambda b, h: (h, 0, 0)),
        ],
        out_specs=pl.BlockSpec((seq, HEAD_DIM), lambda b, h: (b, h)),
        out_shape=jax.ShapeDtypeStruct((t, MEM_HEADS * HEAD_DIM), BF16),
        compiler_params=_params(("parallel", "arbitrary")),
        name="mem_attn",
    )(main, main, mkn, mv, gq, gk, ong)


def _outproj_kernel(og_ref, of_ref, om_ref, x_ref, wo_ref, gm_ref, h_ref, hn_ref):
    n_g = og_ref.shape[1]
    n_f = of_ref.shape[1]
    d = x_ref.shape[1]
    slab = d // OUT_PROJ_SLABS
    sumsq = jnp.zeros((x_ref.shape[0], 1), F32)
    for c in range(OUT_PROJ_SLABS):
        cols = slice(c * slab, (c + 1) * slab)
        acc = _dot(og_ref[...], wo_ref[:n_g, cols])
        acc = acc + _dot(of_ref[...], wo_ref[n_g:n_g + n_f, cols])
        acc = acc + _dot(om_ref[...], wo_ref[n_g + n_f:, cols])
        h = x_ref[:, cols] + acc
        h_ref[:, cols] = h
        sumsq = sumsq + jnp.sum(h * h, axis=-1, keepdims=True)
    inv = lax.rsqrt(sumsq * (1.0 / d) + EPS)
    hn_ref[...] = (h_ref[...] * inv * gm_ref[...]).astype(BF16)


def _out_proj(og, of, om, x2, wo, gm, tm=512):
    t, d = x2.shape
    row = lambda i: (i, 0)
    fixed = lambda i: (0, 0)
    return pl.pallas_call(
        _outproj_kernel,
        grid=(t // tm,),
        in_specs=[
            pl.BlockSpec((tm, og.shape[1]), row),
            pl.BlockSpec((tm, of.shape[1]), row),
            pl.BlockSpec((tm, om.shape[1]), row),
            pl.BlockSpec((tm, d), row),
            pl.BlockSpec(wo.shape, fixed),
            pl.BlockSpec((1, d), fixed),
        ],
        out_specs=[pl.BlockSpec((tm, d), row), pl.BlockSpec((tm, d), row)],
        out_shape=[jax.ShapeDtypeStruct((t, d), F32), jax.ShapeDtypeStruct((t, d), BF16)],
        compiler_params=_params(("parallel",)),
        name="out_proj",
    )(og, of, om, x2, wo, gm)


def _mlp_kernel(hn_ref, h_ref, wu_ref, wd_ref, o_ref):
    @pl.when(pl.program_id(1) == 0)
    def _():
        o_ref[...] = h_ref[...]

    tf = wu_ref.shape[1]
    piece = tf // MLP_SPLIT
    acc = None
    for s in range(MLP_SPLIT):
        cols = slice(s * piece, (s + 1) * piece)
        u = jnp.maximum(_dot(hn_ref[...], wu_ref[:, cols]), 0.0)
        part = _dot((u * u).astype(BF16), wd_ref[cols, :])
        acc = part if acc is None else acc + part
    o_ref[...] += acc


def _mlp(hn, h, wu, wd, tm=512, tf=2048):
    t, d = h.shape
    dff = wu.shape[1]
    blocks = 2 * (2 * d * tf * 2 + tm * d * 2 + 2 * tm * d * 4)
    vmem_limit = blocks + MLP_VMEM_TEMPS
    return pl.pallas_call(
        _mlp_kernel,
        grid=(t // tm, dff // tf),
        in_specs=[
            pl.BlockSpec((tm, d), lambda i, f: (i, 0)),
            pl.BlockSpec((tm, d), lambda i, f: (i, 0)),
            pl.BlockSpec((d, tf), lambda i, f: (0, f)),
            pl.BlockSpec((tf, d), lambda i, f: (f, 0)),
        ],
        out_specs=pl.BlockSpec((tm, d), lambda i, f: (i, 0)),
        out_shape=jax.ShapeDtypeStruct((t, d), F32),
        compiler_params=_params(("parallel", "arbitrary"), vmem_limit),
        name="mlp",
    )(hn, h, wu, wd)


def _regroup_w_in(w_in):
    n_gla = OFF_FQ
    n_fox = OFF_MQ - OFF_FQ
    a0 = n_gla
    f0 = a0 + GLA_RANK
    ff0 = f0 + n_fox
    m0 = ff0 + FOX_HEADS
    wt = w_in.T.astype(BF16)
    q_scale = GLA_DK ** -0.5
    assert math.frexp(q_scale)[0] == 0.5
    n_gq = GLA_HEADS * GLA_DK
    main = jnp.concatenate([wt[:n_gq] * q_scale, wt[n_gq:n_gla], wt[f0:ff0], wt[m0:]], axis=0)
    pad = jnp.zeros((LANES - GLA_RANK - FOX_HEADS, w_in.shape[0]), BF16)
    small = jnp.concatenate([wt[a0:f0], wt[ff0:m0], pad], axis=0)
    return main, small


def kernel(x, mem, attn_norm_g, w_in, gla_a_w2, gla_a_b, fox_f_b, fox_q_norm_g, fox_k_norm_g,
           mem_norm_g, w_mem_kv, mem_q_norm_g, mem_k_norm_g, out_norm_g, w_out, mlp_norm_g,
           w_up, w_down):
    batch, seq, d = x.shape
    n_mem = mem.shape[1]
    t = batch * seq
    x2 = x.reshape(t, d)
    row = lambda a: a.reshape(1, -1).astype(F32)

    w_main, w_small = _regroup_w_in(w_in)
    main, small = _in_proj(x2, row(attn_norm_g), w_main, w_small)

    n_gla = GLA_HEADS * HEAD_DIM
    n_fox = FOX_HEADS * HEAD_DIM
    fb = jnp.zeros((SUBLANES, 1), F32).at[:FOX_HEADS, 0].set(fox_f_b)
    crow, aq, ak = _fox_gate(small, fb, row(fox_q_norm_g), row(fox_k_norm_g), batch, seq)
    ong_fox = out_norm_g[n_gla:n_gla + n_fox].reshape(FOX_HEADS, 1, HEAD_DIM)
    of = _fox(main, aq, ak, crow.reshape(batch, SUBLANES, 1, seq), row(fox_q_norm_g),
              row(fox_k_norm_g), ong_fox, batch, seq)

    mkn, mv = _mem_kv(mem.reshape(batch * n_mem, d), row(mem_norm_g), w_mem_kv.astype(BF16),
                      row(mem_q_norm_g), row(mem_k_norm_g), batch, n_mem)
    ong_mem = out_norm_g[n_gla + n_fox:].reshape(MEM_HEADS, 1, HEAD_DIM)
    om = _mem_attn(main, mkn, mv, row(mem_q_norm_g), row(mem_k_norm_g), ong_mem, batch, seq, n_mem)

    w2p = jnp.zeros((LANES, GLA_HEADS * GLA_DK), F32).at[:GLA_RANK].set(gla_a_w2).astype(BF16)
    og = _gla(main, small, w2p, row(gla_a_b), row(out_norm_g[:n_gla]), batch, seq)

    h, hn = _out_proj(og, of, om, x2, w_out.astype(BF16), row(mlp_norm_g))
    y = _mlp(hn, h, w_up.astype(BF16), w_down.astype(BF16))
    return y.reshape(batch, seq, d)
```

```python
import functools
import math

import jax
import jax.numpy as jnp
import numpy as np
from jax import lax
from jax.experimental import pallas as pl
from jax.experimental.pallas import tpu as pltpu

F32 = jnp.float32
BF16 = jnp.bfloat16

HEAD_DIM = 128
GLA_HEADS = 8
GLA_DK = 64
GLA_RANK = 16
GLA_TAU = 16.0
FOX_HEADS = 4
MEM_HEADS = 4
EPS = 1e-6

LANES = 128
SUBLANES = 8
LOG2E = 1.4426950408889634
GLA_CHUNK = 64
GLA_STEP = 512
GLA_CUMSUM_BLOCK = 256
GLA_SAFE_EXP = 86.0
FOX_TQ = 256
FOX_AUG = 8
FOX_MAX_SPAN = 100.0
MEM_TQ = 512
OUT_PROJ_SLABS = 4
NEG_BIG = -1e30
VMEM_LIMIT = 50 * 1024 * 1024
MLP_VMEM_TEMPS = 5 * 1024 * 1024
MLP_SPLIT = 2

OFF_GQ, OFF_GK, OFF_GV, OFF_GG = 0, 512, 1024, 2048
OFF_FQ, OFF_FK, OFF_FV, OFF_FG = 3072, 3584, 4096, 4608
OFF_MQ, OFF_MG = 5120, 5632
N_MAIN = 6144
SMALL_FF = GLA_RANK
FF_ROW0 = SMALL_FF
assert FF_ROW0 % SUBLANES == 0 and FOX_HEADS <= SUBLANES


def _dot(a, b):
    return jnp.dot(a, b, preferred_element_type=F32)


def _dot_nt(a, b):
    return lax.dot_general(a, b, (((1,), (1,)), ((), ())), preferred_element_type=F32)


def _dot_tn(a, b):
    return lax.dot_general(a, b, (((0,), (0,)), ((), ())), preferred_element_type=F32)


def _rms(x, g):
    return x * lax.rsqrt(jnp.mean(x * x, axis=-1, keepdims=True) + EPS) * g


def _log_sigmoid(z):
    return jnp.minimum(z, 0.0) - jnp.log(1.0 + jnp.exp(-jnp.abs(z)))


def _sigmoid(z):
    return 1.0 / (1.0 + jnp.exp(-z))


def _split3(a):
    hi = a.astype(BF16)
    r = a - hi.astype(F32)
    mid = r.astype(BF16)
    lo = (r - mid.astype(F32)).astype(BF16)
    return hi, mid, lo


def _params(sem, vmem_limit=VMEM_LIMIT):
    return pltpu.CompilerParams(dimension_semantics=sem, vmem_limit_bytes=vmem_limit)


def _inproj_kernel(x_ref, g_ref, wm_ref, ws_ref, main_ref, small_ref, xn_even, xn_odd):
    i = pl.program_id(0)
    j = pl.program_id(1)
    n_col = pl.num_programs(1)
    rs = x_ref.shape[0] // n_col
    odd = i % 2 == 1

    def normalise_slice(dst):
        rows = pl.ds(pl.multiple_of(j * rs, rs), rs)
        dst[rows, :] = _rms(x_ref[rows, :], g_ref[...]).astype(BF16)

    def step(src, dst):
        @pl.when(j == 0)
        def _():
            small_ref[...] = _dot_nt(src[...], ws_ref[...])

        normalise_slice(dst)
        main_ref[...] = _dot_nt(src[...], wm_ref[...]).astype(BF16)

    @pl.when(i == 0)
    def _():
        normalise_slice(xn_even)

    @pl.when(odd)
    def _():
        step(xn_even, xn_odd)

    @pl.when((i > 0) & jnp.logical_not(odd))
    def _():
        step(xn_odd, xn_even)


def _in_proj(x2, g, w_main, w_small, tm=1024, tn=1536):
    t, d = x2.shape
    n_row = t // tm
    prev = lambda i: jnp.maximum(i - 1, 0)
    return pl.pallas_call(
        _inproj_kernel,
        grid=(n_row + 1, N_MAIN // tn),
        in_specs=[
            pl.BlockSpec((tm, d), lambda i, j: (jnp.minimum(i, n_row - 1), 0)),
            pl.BlockSpec((1, d), lambda i, j: (0, 0)),
            pl.BlockSpec((tn, d), lambda i, j: (j, 0)),
            pl.BlockSpec((LANES, d), lambda i, j: (0, 0)),
        ],
        out_specs=[
            pl.BlockSpec((tm, tn), lambda i, j: (prev(i), jnp.where(i > 0, j, 0))),
            pl.BlockSpec((tm, LANES), lambda i, j: (prev(i), 0)),
        ],
        out_shape=[
            jax.ShapeDtypeStruct((t, N_MAIN), BF16),
            jax.ShapeDtypeStruct((t, LANES), F32),
        ],
        scratch_shapes=[pltpu.VMEM((tm, d), BF16), pltpu.VMEM((tm, d), BF16)],
        compiler_params=_params(("arbitrary", "arbitrary")),
        name="in_proj",
    )(x2, g, w_main, w_small)


def _gla_kernel(q_ref, k_ref, v_ref, gg_ref, small_ref, small_next_ref, w2_ref, ab_ref, ong_ref,
                o_ref, st_ref, bc_scr, safe_ref, qs_scr, bs_scr, os_scr):
    C = GLA_CHUNK
    n_pairs = GLA_HEADS // 2
    step = pl.program_id(0) * pl.num_programs(1) + pl.program_id(1)
    slot = step % 2

    @pl.when(pl.program_id(1) == 0)
    def _():
        st_ref[...] = jnp.zeros_like(st_ref)

    def decay_sums(src_ref, dst):
        z = _dot(src_ref[...].astype(BF16), w2_ref[...]) + ab_ref[...]
        la = _log_sigmoid(z) * (LOG2E / GLA_TAU)
        blk = GLA_CUMSUM_BLOCK
        ri = lax.broadcasted_iota(jnp.int32, (blk, blk), 0)
        ci = lax.broadcasted_iota(jnp.int32, (blk, blk), 1)
        same_chunk = (ri ^ ci) < C
        tri = jnp.where(same_chunk, jnp.where(ci <= ri, 1.0, 0.0), 0.0).astype(BF16)
        for i in range(GLA_STEP // blk):
            rows = slice(i * blk, (i + 1) * blk)
            hi, mid, lo = _split3(la[rows, :])
            bc_scr[dst, rows, :] = _dot(tri, hi) + _dot(tri, mid) + _dot(tri, lo)
        safe_ref[dst] = (jnp.min(la) * C >= -GLA_SAFE_EXP).astype(jnp.int32)

    @pl.when(step == 0)
    def _():
        decay_sums(small_ref, 0)

    lane = lax.broadcasted_iota(jnp.int32, (C, LANES), 1)
    low_half = lane < GLA_DK
    st_r = lax.broadcasted_iota(jnp.int32, (2 * HEAD_DIM, LANES), 0)
    st_c = lax.broadcasted_iota(jnp.int32, (2 * HEAD_DIM, LANES), 1)
    st_mask = (st_r < HEAD_DIM) == (st_c < GLA_DK)
    a_r = lax.broadcasted_iota(jnp.int32, (C, 2 * C), 0)
    a_c = lax.broadcasted_iota(jnp.int32, (C, 2 * C), 1)
    causal = a_r >= (a_c & (C - 1))
    zeros_v = jnp.zeros((C, HEAD_DIM), BF16)
    jrow = lax.broadcasted_iota(jnp.int32, (C, LANES), 0)

    def unit_first_half(ch, p, fast):
        rows = slice(ch * C, (ch + 1) * C)
        kl = slice(p * LANES, (p + 1) * LANES)
        vl = slice(p * 2 * HEAD_DIM, (p + 1) * 2 * HEAD_DIM)
        b = bc_scr[slot, rows, kl]
        qs = q_ref[rows, kl].astype(F32)
        kf = k_ref[rows, kl].astype(F32)
        v2 = v_ref[rows, vl]
        qe16 = (qs * jnp.exp2(b)).astype(BF16)
        b_last = b[C - 1:C, :]
        st = st_ref[p]
        o = _dot_nt(qe16, st.astype(BF16))
        if fast:
            ke = kf * jnp.exp2(-b)
            kbd = jnp.concatenate(
                [jnp.where(low_half, ke, 0.0), jnp.where(low_half, 0.0, ke)], axis=0).astype(BF16)
            scores = _dot_nt(qe16, kbd)
            vbd = jnp.concatenate(
                [jnp.concatenate([v2[:, :HEAD_DIM], zeros_v], axis=1),
                 jnp.concatenate([zeros_v, v2[:, HEAD_DIM:]], axis=1)], axis=0)
            upd = _dot_tn(vbd, kbd)
            st_ref[p] = jnp.exp2(b_last) * (st + upd)
            return rows, p, o, scores, vbd
        else:
            kd16 = (kf * jnp.exp2(b_last - b)).astype(BF16)
            upd = _dot_tn(v2, kd16)
            st_ref[p] = jnp.exp2(b_last) * st + jnp.where(st_mask, upd, 0.0)
            qs_scr[...] = qs
            bs_scr[...] = b
            v2f = v2.astype(F32)

            def row(i, carry):
                qi = qs_scr[pl.ds(i, 1), :]
                bi = bs_scr[pl.ds(i, 1), :]
                w = qi * kf * jnp.exp2(jnp.minimum(bi - b, 0.0))
                w = jnp.where(jrow <= i, w, 0.0)
                p0 = jnp.sum(jnp.where(low_half, w, 0.0), axis=1, keepdims=True)
                p1 = jnp.sum(jnp.where(low_half, 0.0, w), axis=1, keepdims=True)
                pv = jnp.concatenate([p0 * v2f[:, :HEAD_DIM], p1 * v2f[:, HEAD_DIM:]], axis=1)
                os_scr[pl.ds(i, 1), :] = jnp.sum(pv, axis=0, keepdims=True)
                return carry

            lax.fori_loop(0, C, row, 0)
            return rows, p, o + os_scr[...], None, None

    def unit_second_half(rows, p, o, scores, vbd):
        if scores is not None:
            o = o + _dot(jnp.where(causal, scores, 0.0).astype(BF16), vbd)
        for hh in range(2):
            hl = slice((2 * p + hh) * HEAD_DIM, (2 * p + hh + 1) * HEAD_DIM)
            on = _rms(o[:, hh * HEAD_DIM:(hh + 1) * HEAD_DIM], ong_ref[:, hl])
            g = gg_ref[rows, hl].astype(F32)
            o_ref[rows, hl] = (on * (g * _sigmoid(g))).astype(BF16)

    def run(fast):
        n_chunks = GLA_STEP // C
        for ch in range(n_chunks):
            if ch == n_chunks // 2:
                decay_sums(small_next_ref, 1 - slot)
            for p in range(n_pairs):
                unit_second_half(*unit_first_half(ch, p, fast))

    safe = safe_ref[slot]

    @pl.when(safe == 1)
    def _():
        run(True)

    @pl.when(safe == 0)
    def _():
        run(False)


def _gla(main, small, w2p, ab, ong, batch, seq):
    t = main.shape[0]
    nc = seq // GLA_STEP
    rb = lambda b, c: b * nc + c
    rb_next = lambda b, c: jnp.minimum(b * nc + c + 1, batch * nc - 1)
    dq = GLA_HEADS * GLA_DK
    dv = GLA_HEADS * HEAD_DIM
    return pl.pallas_call(
        _gla_kernel,
        grid=(batch, nc),
        in_specs=[
            pl.BlockSpec((GLA_STEP, dq), lambda b, c: (rb(b, c), OFF_GQ // dq)),
            pl.BlockSpec((GLA_STEP, dq), lambda b, c: (rb(b, c), OFF_GK // dq)),
            pl.BlockSpec((GLA_STEP, dv), lambda b, c: (rb(b, c), OFF_GV // dv)),
            pl.BlockSpec((GLA_STEP, dv), lambda b, c: (rb(b, c), OFF_GG // dv)),
            pl.BlockSpec((GLA_STEP, LANES), lambda b, c: (rb(b, c), 0)),
            pl.BlockSpec((GLA_STEP, LANES), lambda b, c: (rb_next(b, c), 0)),
            pl.BlockSpec((LANES, dq), lambda b, c: (0, 0)),
            pl.BlockSpec((1, dq), lambda b, c: (0, 0)),
            pl.BlockSpec((1, dv), lambda b, c: (0, 0)),
        ],
        out_specs=pl.BlockSpec((GLA_STEP, dv), lambda b, c: (rb(b, c), 0)),
        out_shape=jax.ShapeDtypeStruct((t, dv), BF16),
        scratch_shapes=[
            pltpu.VMEM((GLA_HEADS // 2, 2 * HEAD_DIM, LANES), F32),
            pltpu.VMEM((2, GLA_STEP, dq), F32),
            pltpu.SMEM((2,), jnp.int32),
            pltpu.VMEM((GLA_CHUNK, LANES), F32),
            pltpu.VMEM((GLA_CHUNK, LANES), F32),
            pltpu.VMEM((GLA_CHUNK, 2 * HEAD_DIM), F32),
        ],
        compiler_params=_params(("arbitrary", "arbitrary")),
        name="gla",
    )(main, main, main, main, small, small, w2p, ab, ong)


def _foxgate_kernel(small_ref, fb_ref, gq_ref, gk_ref, p_ref, rows_ref,
                    crow_ref, aq_ref, ak_ref):
    s = small_ref.shape[0]
    blk = 256
    ff = small_ref[...].T[FF_ROW0:FF_ROW0 + SUBLANES, :]
    lf = _log_sigmoid(ff + fb_ref[...]) * LOG2E
    ri = lax.broadcasted_iota(jnp.int32, (blk, blk), 0)
    ci = lax.broadcasted_iota(jnp.int32, (blk, blk), 1)
    tri = jnp.where(ri <= ci, 1.0, 0.0).astype(BF16)
    carry = jnp.zeros((SUBLANES, 1), F32)
    for i in range(s // blk):
        cols = slice(i * blk, (i + 1) * blk)
        hi, mid, lo = _split3(lf[:, cols])
        c = _dot(hi, tri) + _dot(mid, tri) + _dot(lo, tri) + carry
        crow_ref[:, cols] = c
        carry = c[:, blk - 1:blk]
    qk_bound = _qk_bound(gq_ref[...], gk_ref[...])
    b_hi = qk_bound.astype(BF16).astype(F32)
    b_lo = (qk_bound - b_hi).astype(BF16).astype(F32)
    q_const = rows_ref[0:1, :]
    k_const = rows_ref[1:2, :] + b_hi * rows_ref[2:3, :] + b_lo * rows_ref[3:4, :]
    hi, mid, lo = _split3(crow_ref[...])
    terms = jnp.concatenate([hi.astype(F32), mid.astype(F32), lo.astype(F32),
                             jnp.zeros((SUBLANES, s), F32)], axis=0).astype(BF16)
    aug = _dot_tn(terms, p_ref[...])
    aq_ref[...] = (aug[:, :LANES] + q_const).astype(BF16)
    ak_ref[...] = (aug[:, LANES:] + k_const).astype(BF16)


def _qk_bound(gq, gk):
    gmax = jnp.max(jnp.abs(gq), axis=-1, keepdims=True) * jnp.max(jnp.abs(gk), axis=-1, keepdims=True)
    return gmax * (1.01 * LOG2E * HEAD_DIM ** 0.5)


def _fox_aug_constants():
    p = np.zeros((4 * SUBLANES, 2 * LANES), np.float32)
    rows = np.zeros((SUBLANES, LANES), np.float32)
    for h in range(FOX_HEADS):
        base = FOX_AUG * h
        for j in range(3):
            p[j * SUBLANES + h, base + j] = 1.0
            p[j * SUBLANES + h, LANES + base + 3 + j] = -1.0
            rows[1, base + j] = 1.0
        rows[0, base + 3:base + 8] = 1.0
        rows[2, base + 6] = -1.0
        rows[3, base + 7] = -1.0
    return jnp.asarray(p, BF16), jnp.asarray(rows)


def _fox_gate(small, fb, gq, gk, batch, seq):
    t = small.shape[0]
    p, const_rows = _fox_aug_constants()
    fixed = lambda b: (0, 0)
    return pl.pallas_call(
        _foxgate_kernel,
        grid=(batch,),
        in_specs=[
            pl.BlockSpec((seq, LANES), lambda b: (b, 0)),
            pl.BlockSpec((SUBLANES, 1), fixed),
            pl.BlockSpec((1, HEAD_DIM), fixed),
            pl.BlockSpec((1, HEAD_DIM), fixed),
            pl.BlockSpec(p.shape, fixed),
            pl.BlockSpec(const_rows.shape, fixed),
        ],
        out_specs=[
            pl.BlockSpec((None, SUBLANES, seq), lambda b: (b, 0, 0)),
            pl.BlockSpec((seq, LANES), lambda b: (b, 0)),
            pl.BlockSpec((seq, LANES), lambda b: (b, 0)),
        ],
        out_shape=[
            jax.ShapeDtypeStruct((batch, SUBLANES, seq), F32),
            jax.ShapeDtypeStruct((t, LANES), BF16),
            jax.ShapeDtypeStruct((t, LANES), BF16),
        ],
        compiler_params=_params(("parallel",)),
        name="fox_gate",
    )(small, fb, gq, gk, p, const_rows)


def _fox_kernel(q_ref, k_ref, v_ref, fg_ref, aq_ref, ak_ref, crow_ref, gq_ref, gk_ref, ong_ref,
                o_ref, ka_scr, v1_scr):
    s = q_ref.shape[0]
    h = pl.program_id(1)
    scale = HEAD_DIM ** -0.5 * LOG2E
    lane = lax.broadcasted_iota(jnp.int32, (FOX_TQ, LANES), 1)
    own = (lane >= FOX_AUG * h) & (lane < FOX_AUG * (h + 1))
    qi = lax.broadcasted_iota(jnp.int32, (FOX_TQ, FOX_TQ), 0)
    ki = lax.broadcasted_iota(jnp.int32, (FOX_TQ, FOX_TQ), 1)
    causal = qi >= ki
    ones = jnp.ones((FOX_TQ, HEAD_DIM), BF16)

    def prepare(rows):
        qn = (_rms(q_ref[rows, :].astype(F32), gq_ref[...]) * scale).astype(BF16)
        kn = _rms(k_ref[rows, :].astype(F32), gk_ref[...]).astype(BF16)
        qa = jnp.concatenate(
            [qn, jnp.where(own, aq_ref[rows, :].astype(F32), 0.0).astype(BF16)], axis=1)
        ka = jnp.concatenate([kn, ak_ref[rows, :]], axis=1)
        v1 = jnp.concatenate([v_ref[rows, :], ones], axis=1)
        ka_scr[rows, :] = ka
        v1_scr[rows, :] = v1
        return qa, ka, v1

    def finish(rows, ol):
        o = ol[:, :HEAD_DIM] / ol[:, HEAD_DIM:]
        on = _rms(o, ong_ref[...])
        o_ref[rows, :] = (on * _sigmoid(fg_ref[rows, :].astype(F32))).astype(BF16)

    bounded = 2.0 * _qk_bound(gq_ref[...], gk_ref[...])[0, 0] <= FOX_MAX_SPAN

    @pl.when(bounded)
    def _():
        def logits(qb):
            r0 = qb * FOX_TQ
            rows = slice(r0, r0 + FOX_TQ)
            qa, ka, v1 = prepare(rows)
            diag = jnp.where(causal, _dot_nt(qa, ka), NEG_BIG)
            past = _dot_nt(qa, ka_scr[:r0, :]) if r0 else None
            return rows, r0, diag, past, v1

        def weighted_values(rows, r0, diag, past, v1):
            ol = _dot(jnp.exp2(diag).astype(BF16), v1)
            if r0:
                ol = ol + _dot(jnp.exp2(past).astype(BF16), v1_scr[:r0, :])
            finish(rows, ol)

        n_blocks = s // FOX_TQ
        pending = logits(0)
        for qb in range(n_blocks):
            nxt = logits(qb + 1) if qb + 1 < n_blocks else None
            weighted_values(*pending)
            pending = nxt

    @pl.when(jnp.logical_not(bounded))
    def _():
        for qb in range(s // FOX_TQ):
            r0 = qb * FOX_TQ
            rows = slice(r0, r0 + FOX_TQ)
            qa, ka, v1 = prepare(rows)
            qn = qa[:, :HEAD_DIM]
            diag = _dot_nt(qn, ka[:, :HEAD_DIM]) - crow_ref[:, rows]
            diag = jnp.where(causal, diag, NEG_BIG)
            m = jnp.max(diag, axis=-1, keepdims=True)
            if r0:
                past = _dot_nt(qn, ka_scr[:r0, :HEAD_DIM]) - crow_ref[:, :r0]
                m = jnp.maximum(m, jnp.max(past, axis=-1, keepdims=True))
                ol = _dot(jnp.exp2(past - m).astype(BF16), v1_scr[:r0, :])
                ol = ol + _dot(jnp.exp2(diag - m).astype(BF16), v1)
            else:
                ol = _dot(jnp.exp2(diag - m).astype(BF16), v1)
            finish(rows, ol)


def _fox(main, aq, ak, crow4, gq, gk, ong, batch, seq):
    t = main.shape[0]
    cb = lambda off: off // HEAD_DIM
    return pl.pallas_call(
        _fox_kernel,
        grid=(batch, FOX_HEADS),
        in_specs=[
            pl.BlockSpec((seq, HEAD_DIM), lambda b, h: (b, cb(OFF_FQ) + h)),
            pl.BlockSpec((seq, HEAD_DIM), lambda b, h: (b, cb(OFF_FK) + h)),
            pl.BlockSpec((seq, HEAD_DIM), lambda b, h: (b, cb(OFF_FV) + h)),
            pl.BlockSpec((seq, HEAD_DIM), lambda b, h: (b, cb(OFF_FG) + h)),
            pl.BlockSpec((seq, LANES), lambda b, h: (b, 0)),
            pl.BlockSpec((seq, LANES), lambda b, h: (b, 0)),
            pl.BlockSpec((None, None, 1, seq), lambda b, h: (b, h, 0, 0)),
            pl.BlockSpec((1, HEAD_DIM), lambda b, h: (0, 0)),
            pl.BlockSpec((1, HEAD_DIM), lambda b, h: (0, 0)),
            pl.BlockSpec((None, 1, HEAD_DIM), lambda b, h: (h, 0, 0)),
        ],
        out_specs=pl.BlockSpec((seq, HEAD_DIM), lambda b, h: (b, h)),
        out_shape=jax.ShapeDtypeStruct((t, FOX_HEADS * HEAD_DIM), BF16),
        scratch_shapes=[
            pltpu.VMEM((seq, HEAD_DIM + LANES), BF16),
            pltpu.VMEM((seq, 2 * HEAD_DIM), BF16),
        ],
        compiler_params=_params(("parallel", "arbitrary")),
        name="fox",
    )(main, main, main, main, aq, ak, crow4, gq, gk, ong)


def _memkv_kernel(mem_ref, g_ref, w_ref, gq_ref, gk_ref, mk_ref, mv_ref):
    mn = _rms(mem_ref[...], g_ref[...]).astype(BF16)
    kv = _dot(mn, w_ref[...])
    n_mem = kv.shape[0]
    dk = MEM_HEADS * HEAD_DIM
    ones = jnp.ones((n_mem, HEAD_DIM), BF16)
    qk_bound = _qk_bound(gq_ref[...], gk_ref[...])
    b_hi = qk_bound.astype(BF16).astype(F32)
    b_lo = (qk_bound - b_hi).astype(BF16).astype(F32)
    lane = lax.broadcasted_iota(jnp.int32, (n_mem, LANES), 1)
    shift = jnp.where(lane == 0, -b_hi, jnp.where(lane == 1, -b_lo, 0.0)).astype(BF16)
    for hh in range(MEM_HEADS):
        hl = slice(hh * HEAD_DIM, (hh + 1) * HEAD_DIM)
        mk_ref[:, 2 * hh * HEAD_DIM:(2 * hh + 1) * HEAD_DIM] = _rms(kv[:, hl], gk_ref[...]).astype(BF16)
        mk_ref[:, (2 * hh + 1) * HEAD_DIM:(2 * hh + 2) * HEAD_DIM] = shift
        mv_ref[:, 2 * hh * HEAD_DIM:(2 * hh + 1) * HEAD_DIM] = kv[:, dk + hh * HEAD_DIM:dk + (hh + 1) * HEAD_DIM].astype(BF16)
        mv_ref[:, (2 * hh + 1) * HEAD_DIM:(2 * hh + 2) * HEAD_DIM] = ones


def _mem_kv(mem2, g, w, gq, gk, batch, n_mem):
    d = mem2.shape[1]
    dk = MEM_HEADS * HEAD_DIM
    return pl.pallas_call(
        _memkv_kernel,
        grid=(batch,),
        in_specs=[
            pl.BlockSpec((n_mem, d), lambda b: (b, 0)),
            pl.BlockSpec((1, d), lambda b: (0, 0)),
            pl.BlockSpec((d, 2 * dk), lambda b: (0, 0)),
            pl.BlockSpec((1, HEAD_DIM), lambda b: (0, 0)),
            pl.BlockSpec((1, HEAD_DIM), lambda b: (0, 0)),
        ],
        out_specs=[
            pl.BlockSpec((n_mem, 2 * dk), lambda b: (b, 0)),
            pl.BlockSpec((n_mem, 2 * dk), lambda b: (b, 0)),
        ],
        out_shape=[
            jax.ShapeDtypeStruct((batch * n_mem, 2 * dk), BF16),
            jax.ShapeDtypeStruct((batch * n_mem, 2 * dk), BF16),
        ],
        compiler_params=_params(("parallel",)),
        name="mem_kv",
    )(mem2, g, w, gq, gk)


def _memattn_kernel(q_ref, mg_ref, mk_ref, mv_ref, gq_ref, gk_ref, ong_ref, o_ref):
    s = q_ref.shape[0]
    scale = HEAD_DIM ** -0.5 * LOG2E
    lane = lax.broadcasted_iota(jnp.int32, (MEM_TQ, LANES), 1)
    q_extra = jnp.where(lane < 2, 1.0, 0.0).astype(BF16)

    def q_block(rows):
        return (_rms(q_ref[rows, :].astype(F32), gq_ref[...]) * scale).astype(BF16)

    def finish(rows, ol):
        o = ol[:, :HEAD_DIM] / ol[:, HEAD_DIM:]
        on = _rms(o, ong_ref[...])
        o_ref[rows, :] = (on * _sigmoid(mg_ref[rows, :].astype(F32))).astype(BF16)

    bounded = 2.0 * _qk_bound(gq_ref[...], gk_ref[...])[0, 0] <= FOX_MAX_SPAN

    @pl.when(bounded)
    def _():
        def logits(qb):
            rows = slice(qb * MEM_TQ, (qb + 1) * MEM_TQ)
            qa = jnp.concatenate([q_block(rows), q_extra], axis=1)
            return rows, _dot_nt(qa, mk_ref[...])

        n_blocks = s // MEM_TQ
        pending = logits(0)
        for qb in range(n_blocks):
            nxt = logits(qb + 1) if qb + 1 < n_blocks else None
            rows, lg = pending
            finish(rows, _dot(jnp.exp2(lg).astype(BF16), mv_ref[...]))
            pending = nxt

    @pl.when(jnp.logical_not(bounded))
    def _():
        for qb in range(s // MEM_TQ):
            rows = slice(qb * MEM_TQ, (qb + 1) * MEM_TQ)
            logits = _dot_nt(q_block(rows), mk_ref[:, :HEAD_DIM])
            m = jnp.max(logits, axis=-1, keepdims=True)
            finish(rows, _dot(jnp.exp2(logits - m).astype(BF16), mv_ref[...]))


def _mem_attn(main, mkn, mv, gq, gk, ong, batch, seq, n_mem):
    t = main.shape[0]
    cb = lambda off: off // HEAD_DIM
    return pl.pallas_call(
        _memattn_kernel,
        grid=(batch, MEM_HEADS),
        in_specs=[
            pl.BlockSpec((seq, HEAD_DIM), lambda b, h: (b, cb(OFF_MQ) + h)),
            pl.BlockSpec((seq, HEAD_DIM), lambda b, h: (b, cb(OFF_MG) + h)),
            pl.BlockSpec((n_mem, 2 * HEAD_DIM), lambda b, h: (b, h)),
            pl.BlockSpec((n_mem, 2 * HEAD_DIM), lambda b, h: (b, h)),
            pl.BlockSpec((1, HEAD_DIM), lambda b, h: (0, 0)),
            pl.BlockSpec((1, HEAD_DIM), lambda b, h: (0, 0)),
            pl.BlockSpec((None, 1, HEAD_DIM), lambda b, h: (h, 0, 0)),
        ],
        out_specs=pl.BlockSpec((seq, HEAD_DIM), lambda b, h: (b, h)),
        out_shape=jax.ShapeDtypeStruct((t, MEM_HEADS * HEAD_DIM), BF16),
        compiler_params=_params(("parallel", "arbitrary")),
        name="mem_attn",
    )(main, main, mkn, mv, gq, gk, ong)


def _outproj_kernel(og_ref, of_ref, om_ref, x_ref, wo_ref, gm_ref, h_ref, hn_ref):
    n_g = og_ref.shape[1]
    n_f = of_ref.shape[1]
    d = x_ref.shape[1]
    slab = d // OUT_PROJ_SLABS
    sumsq = jnp.zeros((x_ref.shape[0], 1), F32)
    for c in range(OUT_PROJ_SLABS):
        cols = slice(c * slab, (c + 1) * slab)
        acc = _dot(og_ref[...], wo_ref[:n_g, cols])
        acc = acc + _dot(of_ref[...], wo_ref[n_g:n_g + n_f, cols])
        acc = acc + _dot(om_ref[...], wo_ref[n_g + n_f:, cols])
        h = x_ref[:, cols] + acc
        h_ref[:, cols] = h
        sumsq = sumsq + jnp.sum(h * h, axis=-1, keepdims=True)
    inv = lax.rsqrt(sumsq * (1.0 / d) + EPS)
    hn_ref[...] = (h_ref[...] * inv * gm_ref[...]).astype(BF16)


def _out_proj(og, of, om, x2, wo, gm, tm=512):
    t, d = x2.shape
    row = lambda i: (i, 0)
    fixed = lambda i: (0, 0)
    return pl.pallas_call(
        _outproj_kernel,
        grid=(t // tm,),
        in_specs=[
            pl.BlockSpec((tm, og.shape[1]), row),
            pl.BlockSpec((tm, of.shape[1]), row),
            pl.BlockSpec((tm, om.shape[1]), row),
            pl.BlockSpec((tm, d), row),
            pl.BlockSpec(wo.shape, fixed),
            pl.BlockSpec((1, d), fixed),
        ],
        out_specs=[pl.BlockSpec((tm, d), row), pl.BlockSpec((tm, d), row)],
        out_shape=[jax.ShapeDtypeStruct((t, d), F32), jax.ShapeDtypeStruct((t, d), BF16)],
        compiler_params=_params(("parallel",)),
        name="out_proj",
    )(og, of, om, x2, wo, gm)


def _mlp_kernel(hn_ref, h_ref, wu_ref, wd_ref, o_ref):
    def hidden_contribution():
        tf = wu_ref.shape[1]
        piece = tf // MLP_SPLIT
        acc = None
        for s in range(MLP_SPLIT):
            cols = slice(s * piece, (s + 1) * piece)
            u = jnp.maximum(_dot(hn_ref[...], wu_ref[:, cols]), 0.0)
            part = _dot((u * u).astype(BF16), wd_ref[cols, :])
            acc = part if acc is None else acc + part
        return acc

    @pl.when(pl.program_id(1) == 0)
    def _():
        o_ref[...] = h_ref[...] + hidden_contribution()

    @pl.when(pl.program_id(1) > 0)
    def _():
        o_ref[...] += hidden_contribution()


def _mlp(hn, h, wu, wd, tm=512, tf=2048):
    t, d = h.shape
    dff = wu.shape[1]
    blocks = 2 * (2 * d * tf * 2 + tm * d * 2 + 2 * tm * d * 4)
    vmem_limit = blocks + MLP_VMEM_TEMPS
    return pl.pallas_call(
        _mlp_kernel,
        grid=(t // tm, dff // tf),
        in_specs=[
            pl.BlockSpec((tm, d), lambda i, f: (i, 0)),
            pl.BlockSpec((tm, d), lambda i, f: (i, 0)),
            pl.BlockSpec((d, tf), lambda i, f: (0, f)),
            pl.BlockSpec((tf, d), lambda i, f: (f, 0)),
        ],
        out_specs=pl.BlockSpec((tm, d), lambda i, f: (i, 0)),
        out_shape=jax.ShapeDtypeStruct((t, d), F32),
        compiler_params=_params(("parallel", "arbitrary"), vmem_limit),
        name="mlp",
    )(hn, h, wu, wd)


def _regroup_kernel(wt_ref, main_ref, small_ref):
    n_gla = OFF_FQ
    n_fox = OFF_MQ - OFF_FQ
    a0 = n_gla
    f0 = a0 + GLA_RANK
    ff0 = f0 + n_fox
    m0 = ff0 + FOX_HEADS
    n_gq = GLA_HEADS * GLA_DK
    q_scale = GLA_DK ** -0.5
    assert math.frexp(q_scale)[0] == 0.5
    main_ref[:n_gq, :] = (wt_ref[:n_gq, :] * q_scale).astype(BF16)
    main_ref[n_gq:n_gla, :] = wt_ref[n_gq:n_gla, :].astype(BF16)
    main_ref[OFF_FQ:OFF_MQ, :] = wt_ref[f0:ff0, :].astype(BF16)
    main_ref[OFF_MQ:, :] = wt_ref[m0:, :].astype(BF16)
    pad = jnp.zeros((LANES - GLA_RANK - FOX_HEADS, wt_ref.shape[1]), F32)
    small_ref[...] = jnp.concatenate(
        [wt_ref[a0:f0, :], wt_ref[ff0:m0, :], pad], axis=0).astype(BF16)


def _regroup_w_in(w_in, tc=256):
    wt = w_in.T
    n_in, d = wt.shape
    return pl.pallas_call(
        _regroup_kernel,
        grid=(d // tc,),
        in_specs=[pl.BlockSpec((n_in, tc), lambda i: (0, i))],
        out_specs=[pl.BlockSpec((N_MAIN, tc), lambda i: (0, i)),
                   pl.BlockSpec((LANES, tc), lambda i: (0, i))],
        out_shape=[jax.ShapeDtypeStruct((N_MAIN, d), BF16), jax.ShapeDtypeStruct((LANES, d), BF16)],
        compiler_params=_params(("parallel",)),
        name="regroup_w_in",
    )(wt)


def kernel(x, mem, attn_norm_g, w_in, gla_a_w2, gla_a_b, fox_f_b, fox_q_norm_g, fox_k_norm_g,
           mem_norm_g, w_mem_kv, mem_q_norm_g, mem_k_norm_g, out_norm_g, w_out, mlp_norm_g,
           w_up, w_down):
    batch, seq, d = x.shape
    n_mem = mem.shape[1]
    t = batch * seq
    x2 = x.reshape(t, d)
    row = lambda a: a.reshape(1, -1).astype(F32)

    w_main, w_small = _regroup_w_in(w_in)
    main, small = _in_proj(x2, row(attn_norm_g), w_main, w_small)

    n_gla = GLA_HEADS * HEAD_DIM
    n_fox = FOX_HEADS * HEAD_DIM
    fb = jnp.zeros((SUBLANES, 1), F32).at[:FOX_HEADS, 0].set(fox_f_b)
    crow, aq, ak = _fox_gate(small, fb, row(fox_q_norm_g), row(fox_k_norm_g), batch, seq)
    ong_fox = out_norm_g[n_gla:n_gla + n_fox].reshape(FOX_HEADS, 1, HEAD_DIM)
    of = _fox(main, aq, ak, crow.reshape(batch, SUBLANES, 1, seq), row(fox_q_norm_g),
              row(fox_k_norm_g), ong_fox, batch, seq)

    mkn, mv = _mem_kv(mem.reshape(batch * n_mem, d), row(mem_norm_g), w_mem_kv.astype(BF16),
                      row(mem_q_norm_g), row(mem_k_norm_g), batch, n_mem)
    ong_mem = out_norm_g[n_gla + n_fox:].reshape(MEM_HEADS, 1, HEAD_DIM)
    om = _mem_attn(main, mkn, mv, row(mem_q_norm_g), row(mem_k_norm_g), ong_mem, batch, seq, n_mem)

    w2p = jnp.zeros((LANES, GLA_HEADS * GLA_DK), F32).at[:GLA_RANK].set(gla_a_w2).astype(BF16)
    og = _gla(main, small, w2p, row(gla_a_b), row(out_norm_g[:n_gla]), batch, seq)

    h, hn = _out_proj(og, of, om, x2, w_out.astype(BF16), row(mlp_norm_g))
    y = _mlp(hn, h, w_up.astype(BF16), w_down.astype(BF16))
    return y.reshape(batch, seq, d)
```

```python
import functools
import math

import jax
import jax.numpy as jnp
import numpy as np
from jax import lax
from jax.experimental import pallas as pl
from jax.experimental.pallas import tpu as pltpu

F32 = jnp.float32
BF16 = jnp.bfloat16

HEAD_DIM = 128
GLA_HEADS = 8
GLA_DK = 64
GLA_RANK = 16
GLA_TAU = 16.0
FOX_HEADS = 4
MEM_HEADS = 4
EPS = 1e-6

LANES = 128
SUBLANES = 8
LOG2E = 1.4426950408889634
GLA_CHUNK = 64
GLA_STEP = 512
GLA_CUMSUM_BLOCK = 256
GLA_SAFE_EXP = 86.0
FOX_TQ = 256
FOX_AUG = 8
FOX_MAX_SPAN = 100.0
MEM_TQ = 512
OUT_PROJ_SLABS = 4
NEG_BIG = -1e30
VMEM_LIMIT = 50 * 1024 * 1024
MLP_VMEM_TEMPS = 5 * 1024 * 1024
MLP_SPLIT = 2
INPROJ_VMEM_TEMPS = 8 * 1024 * 1024

OFF_GQ, OFF_GK, OFF_GV, OFF_GG = 0, 512, 1024, 2048
OFF_FQ, OFF_FK, OFF_FV, OFF_FG = 3072, 3584, 4096, 4608
OFF_MQ, OFF_MG = 5120, 5632
N_MAIN = 6144
SMALL_FF = GLA_RANK
FF_ROW0 = SMALL_FF
assert FF_ROW0 % SUBLANES == 0 and FOX_HEADS <= SUBLANES


def _dot(a, b):
    return jnp.dot(a, b, preferred_element_type=F32)


def _dot_nt(a, b):
    return lax.dot_general(a, b, (((1,), (1,)), ((), ())), preferred_element_type=F32)


def _dot_tn(a, b):
    return lax.dot_general(a, b, (((0,), (0,)), ((), ())), preferred_element_type=F32)


def _rms(x, g):
    return x * lax.rsqrt(jnp.mean(x * x, axis=-1, keepdims=True) + EPS) * g


def _log_sigmoid(z):
    return jnp.minimum(z, 0.0) - jnp.log(1.0 + jnp.exp(-jnp.abs(z)))


def _sigmoid(z):
    return 1.0 / (1.0 + jnp.exp(-z))


def _split3(a):
    hi = a.astype(BF16)
    r = a - hi.astype(F32)
    mid = r.astype(BF16)
    lo = (r - mid.astype(F32)).astype(BF16)
    return hi, mid, lo


def _params(sem, vmem_limit=VMEM_LIMIT):
    return pltpu.CompilerParams(dimension_semantics=sem, vmem_limit_bytes=vmem_limit)


def _inproj_kernel(*refs, n_riders):
    x_ref, g_ref, wm_ref, ws_ref = refs[:4]
    rider_in = refs[4:4 + n_riders]
    main_ref, small_ref = refs[4 + n_riders:6 + n_riders]
    rider_out = refs[6 + n_riders:6 + 2 * n_riders]
    xn_even, xn_odd = refs[6 + 2 * n_riders:]
    i = pl.program_id(0)
    j = pl.program_id(1)
    n_col = pl.num_programs(1)
    rs = x_ref.shape[0] // n_col
    odd = i % 2 == 1

    def normalise_slice(dst):
        rows = pl.ds(pl.multiple_of(j * rs, rs), rs)
        dst[rows, :] = _rms(x_ref[rows, :], g_ref[...]).astype(BF16)

    def step(src, dst):
        @pl.when(j == 0)
        def _():
            small_ref[...] = _dot_nt(src[...], ws_ref[...])

        normalise_slice(dst)
        for w32, w16 in zip(rider_in, rider_out):
            w16[...] = w32[...].astype(BF16)
        main_ref[...] = _dot_nt(src[...], wm_ref[...]).astype(BF16)

    @pl.when(i == 0)
    def _():
        normalise_slice(xn_even)

    @pl.when(odd)
    def _():
        step(xn_even, xn_odd)

    @pl.when((i > 0) & jnp.logical_not(odd))
    def _():
        step(xn_odd, xn_even)


def _in_proj(x2, g, w_main, w_small, riders, tm=1024, tn=1536):
    t, d = x2.shape
    n_row = t // tm
    n_col = N_MAIN // tn
    n_steps = n_row * n_col
    prev = lambda i: jnp.maximum(i - 1, 0)
    slab = lambda i, j: (jnp.where(i > 0, (i - 1) * n_col + j, 0), 0)
    rider_specs = []
    rider_bytes = 0
    for w in riders:
        assert w.shape[0] % (n_steps * 2 * SUBLANES) == 0
        rider_specs.append(pl.BlockSpec((w.shape[0] // n_steps, w.shape[1]), slab))
        rider_bytes += w.size // n_steps * (4 + 2)
    blocks = 2 * (tm * d * 4 + tn * d * 2 + LANES * d * 2 + tm * tn * 2 + tm * LANES * 4 + rider_bytes)
    vmem_limit = blocks + 2 * tm * d * 2 + INPROJ_VMEM_TEMPS
    return pl.pallas_call(
        functools.partial(_inproj_kernel, n_riders=len(riders)),
        grid=(n_row + 1, n_col),
        in_specs=[
            pl.BlockSpec((tm, d), lambda i, j: (jnp.minimum(i, n_row - 1), 0)),
            pl.BlockSpec((1, d), lambda i, j: (0, 0)),
            pl.BlockSpec((tn, d), lambda i, j: (j, 0)),
            pl.BlockSpec((LANES, d), lambda i, j: (0, 0)),
        ] + rider_specs,
        out_specs=[
            pl.BlockSpec((tm, tn), lambda i, j: (prev(i), jnp.where(i > 0, j, 0))),
            pl.BlockSpec((tm, LANES), lambda i, j: (prev(i), 0)),
        ] + rider_specs,
        out_shape=[
            jax.ShapeDtypeStruct((t, N_MAIN), BF16),
            jax.ShapeDtypeStruct((t, LANES), F32),
        ] + [jax.ShapeDtypeStruct(w.shape, BF16) for w in riders],
        scratch_shapes=[pltpu.VMEM((tm, d), BF16), pltpu.VMEM((tm, d), BF16)],
        compiler_params=_params(("arbitrary", "arbitrary"), vmem_limit),
        name="in_proj",
    )(x2, g, w_main, w_small, *riders)


def _gla_kernel(q_ref, k_ref, v_ref, gg_ref, small_ref, small_next_ref, w2_ref, ab_ref, ong_ref,
                o_ref, st_ref, bc_scr, safe_ref, qs_scr, bs_scr, os_scr):
    C = GLA_CHUNK
    n_pairs = GLA_HEADS // 2
    step = pl.program_id(0) * pl.num_programs(1) + pl.program_id(1)
    slot = step % 2

    @pl.when(pl.program_id(1) == 0)
    def _():
        st_ref[...] = jnp.zeros_like(st_ref)

    def decay_sums(src_ref, dst):
        z = _dot(src_ref[...].astype(BF16), w2_ref[...]) + ab_ref[...]
        la = _log_sigmoid(z) * (LOG2E / GLA_TAU)
        blk = GLA_CUMSUM_BLOCK
        ri = lax.broadcasted_iota(jnp.int32, (blk, blk), 0)
        ci = lax.broadcasted_iota(jnp.int32, (blk, blk), 1)
        same_chunk = (ri ^ ci) < C
        tri = jnp.where(same_chunk, jnp.where(ci <= ri, 1.0, 0.0), 0.0).astype(BF16)
        for i in range(GLA_STEP // blk):
            rows = slice(i * blk, (i + 1) * blk)
            hi, mid, lo = _split3(la[rows, :])
            bc_scr[dst, rows, :] = _dot(tri, hi) + _dot(tri, mid) + _dot(tri, lo)
        safe_ref[dst] = (jnp.min(la) * C >= -GLA_SAFE_EXP).astype(jnp.int32)

    @pl.when(step == 0)
    def _():
        decay_sums(small_ref, 0)

    lane = lax.broadcasted_iota(jnp.int32, (C, LANES), 1)
    low_half = lane < GLA_DK
    st_r = lax.broadcasted_iota(jnp.int32, (2 * HEAD_DIM, LANES), 0)
    st_c = lax.broadcasted_iota(jnp.int32, (2 * HEAD_DIM, LANES), 1)
    st_mask = (st_r < HEAD_DIM) == (st_c < GLA_DK)
    a_r = lax.broadcasted_iota(jnp.int32, (C, 2 * C), 0)
    a_c = lax.broadcasted_iota(jnp.int32, (C, 2 * C), 1)
    causal = a_r >= (a_c & (C - 1))
    zeros_v = jnp.zeros((C, HEAD_DIM), BF16)
    jrow = lax.broadcasted_iota(jnp.int32, (C, LANES), 0)

    def unit_first_half(ch, p, fast):
        rows = slice(ch * C, (ch + 1) * C)
        kl = slice(p * LANES, (p + 1) * LANES)
        vl = slice(p * 2 * HEAD_DIM, (p + 1) * 2 * HEAD_DIM)
        b = bc_scr[slot, rows, kl]
        qs = q_ref[rows, kl].astype(F32)
        kf = k_ref[rows, kl].astype(F32)
        v2 = v_ref[rows, vl]
        qe16 = (qs * jnp.exp2(b)).astype(BF16)
        b_last = b[C - 1:C, :]
        st = st_ref[p]
        o = _dot_nt(qe16, st.astype(BF16))
        if fast:
            ke = kf * jnp.exp2(-b)
            kbd = jnp.concatenate(
                [jnp.where(low_half, ke, 0.0), jnp.where(low_half, 0.0, ke)], axis=0).astype(BF16)
            scores = _dot_nt(qe16, kbd)
            vbd = jnp.concatenate(
                [jnp.concatenate([v2[:, :HEAD_DIM], zeros_v], axis=1),
                 jnp.concatenate([zeros_v, v2[:, HEAD_DIM:]], axis=1)], axis=0)
            upd = _dot_tn(vbd, kbd)
            st_ref[p] = jnp.exp2(b_last) * (st + upd)
            return rows, p, o, scores, vbd
        else:
            kd16 = (kf * jnp.exp2(b_last - b)).astype(BF16)
            upd = _dot_tn(v2, kd16)
            st_ref[p] = jnp.exp2(b_last) * st + jnp.where(st_mask, upd, 0.0)
            qs_scr[...] = qs
            bs_scr[...] = b
            v2f = v2.astype(F32)

            def row(i, carry):
                qi = qs_scr[pl.ds(i, 1), :]
                bi = bs_scr[pl.ds(i, 1), :]
                w = qi * kf * jnp.exp2(jnp.minimum(bi - b, 0.0))
                w = jnp.where(jrow <= i, w, 0.0)
                p0 = jnp.sum(jnp.where(low_half, w, 0.0), axis=1, keepdims=True)
                p1 = jnp.sum(jnp.where(low_half, 0.0, w), axis=1, keepdims=True)
                pv = jnp.concatenate([p0 * v2f[:, :HEAD_DIM], p1 * v2f[:, HEAD_DIM:]], axis=1)
                os_scr[pl.ds(i, 1), :] = jnp.sum(pv, axis=0, keepdims=True)
                return carry

            lax.fori_loop(0, C, row, 0)
            return rows, p, o + os_scr[...], None, None

    def unit_second_half(rows, p, o, scores, vbd):
        if scores is not None:
            o = o + _dot(jnp.where(causal, scores, 0.0).astype(BF16), vbd)
        for hh in range(2):
            hl = slice((2 * p + hh) * HEAD_DIM, (2 * p + hh + 1) * HEAD_DIM)
            on = _rms(o[:, hh * HEAD_DIM:(hh + 1) * HEAD_DIM], ong_ref[:, hl])
            g = gg_ref[rows, hl].astype(F32)
            o_ref[rows, hl] = (on * (g * _sigmoid(g))).astype(BF16)

    def run(fast):
        n_chunks = GLA_STEP // C
        for ch in range(n_chunks):
            if ch == n_chunks // 2:
                decay_sums(small_next_ref, 1 - slot)
            for p in range(n_pairs):
                unit_second_half(*unit_first_half(ch, p, fast))

    safe = safe_ref[slot]

    @pl.when(safe == 1)
    def _():
        run(True)

    @pl.when(safe == 0)
    def _():
        run(False)


def _gla(main, small, w2p, ab, ong, batch, seq):
    t = main.shape[0]
    nc = seq // GLA_STEP
    rb = lambda b, c: b * nc + c
    rb_next = lambda b, c: jnp.minimum(b * nc + c + 1, batch * nc - 1)
    dq = GLA_HEADS * GLA_DK
    dv = GLA_HEADS * HEAD_DIM
    return pl.pallas_call(
        _gla_kernel,
        grid=(batch, nc),
        in_specs=[
            pl.BlockSpec((GLA_STEP, dq), lambda b, c: (rb(b, c), OFF_GQ // dq)),
            pl.BlockSpec((GLA_STEP, dq), lambda b, c: (rb(b, c), OFF_GK // dq)),
            pl.BlockSpec((GLA_STEP, dv), lambda b, c: (rb(b, c), OFF_GV // dv)),
            pl.BlockSpec((GLA_STEP, dv), lambda b, c: (rb(b, c), OFF_GG // dv)),
            pl.BlockSpec((GLA_STEP, LANES), lambda b, c: (rb(b, c), 0)),
            pl.BlockSpec((GLA_STEP, LANES), lambda b, c: (rb_next(b, c), 0)),
            pl.BlockSpec((LANES, dq), lambda b, c: (0, 0)),
            pl.BlockSpec((1, dq), lambda b, c: (0, 0)),
            pl.BlockSpec((1, dv), lambda b, c: (0, 0)),
        ],
        out_specs=pl.BlockSpec((GLA_STEP, dv), lambda b, c: (rb(b, c), 0)),
        out_shape=jax.ShapeDtypeStruct((t, dv), BF16),
        scratch_shapes=[
            pltpu.VMEM((GLA_HEADS // 2, 2 * HEAD_DIM, LANES), F32),
            pltpu.VMEM((2, GLA_STEP, dq), F32),
            pltpu.SMEM((2,), jnp.int32),
            pltpu.VMEM((GLA_CHUNK, LANES), F32),
            pltpu.VMEM((GLA_CHUNK, LANES), F32),
            pltpu.VMEM((GLA_CHUNK, 2 * HEAD_DIM), F32),
        ],
        compiler_params=_params(("arbitrary", "arbitrary")),
        name="gla",
    )(main, main, main, main, small, small, w2p, ab, ong)


def _foxgate_kernel(small_ref, fb_ref, gq_ref, gk_ref, p_ref, rows_ref,
                    crow_ref, aq_ref, ak_ref):
    s = small_ref.shape[0]
    blk = 256
    ff = small_ref[...].T[FF_ROW0:FF_ROW0 + SUBLANES, :]
    lf = _log_sigmoid(ff + fb_ref[...]) * LOG2E
    ri = lax.broadcasted_iota(jnp.int32, (blk, blk), 0)
    ci = lax.broadcasted_iota(jnp.int32, (blk, blk), 1)
    tri = jnp.where(ri <= ci, 1.0, 0.0).astype(BF16)
    carry = jnp.zeros((SUBLANES, 1), F32)
    for i in range(s // blk):
        cols = slice(i * blk, (i + 1) * blk)
        hi, mid, lo = _split3(lf[:, cols])
        c = _dot(hi, tri) + _dot(mid, tri) + _dot(lo, tri) + carry
        crow_ref[:, cols] = c
        carry = c[:, blk - 1:blk]
    qk_bound = _qk_bound(gq_ref[...], gk_ref[...])
    b_hi = qk_bound.astype(BF16).astype(F32)
    b_lo = (qk_bound - b_hi).astype(BF16).astype(F32)
    q_const = rows_ref[0:1, :]
    k_const = rows_ref[1:2, :] + b_hi * rows_ref[2:3, :] + b_lo * rows_ref[3:4, :]
    hi, mid, lo = _split3(crow_ref[...])
    terms = jnp.concatenate([hi.astype(F32), mid.astype(F32), lo.astype(F32),
                             jnp.zeros((SUBLANES, s), F32)], axis=0).astype(BF16)
    aug = _dot_tn(terms, p_ref[...])
    aq_ref[...] = (aug[:, :LANES] + q_const).astype(BF16)
    ak_ref[...] = (aug[:, LANES:] + k_const).astype(BF16)


def _qk_bound(gq, gk):
    gmax = jnp.max(jnp.abs(gq), axis=-1, keepdims=True) * jnp.max(jnp.abs(gk), axis=-1, keepdims=True)
    return gmax * (1.01 * LOG2E * HEAD_DIM ** 0.5)


def _fox_aug_constants():
    p = np.zeros((4 * SUBLANES, 2 * LANES), np.float32)
    rows = np.zeros((SUBLANES, LANES), np.float32)
    for h in range(FOX_HEADS):
        base = FOX_AUG * h
        for j in range(3):
            p[j * SUBLANES + h, base + j] = 1.0
            p[j * SUBLANES + h, LANES + base + 3 + j] = -1.0
            rows[1, base + j] = 1.0
        rows[0, base + 3:base + 8] = 1.0
        rows[2, base + 6] = -1.0
        rows[3, base + 7] = -1.0
    return jnp.asarray(p, BF16), jnp.asarray(rows)


def _fox_gate(small, fb, gq, gk, batch, seq):
    t = small.shape[0]
    p, const_rows = _fox_aug_constants()
    fixed = lambda b: (0, 0)
    return pl.pallas_call(
        _foxgate_kernel,
        grid=(batch,),
        in_specs=[
            pl.BlockSpec((seq, LANES), lambda b: (b, 0)),
            pl.BlockSpec((SUBLANES, 1), fixed),
            pl.BlockSpec((1, HEAD_DIM), fixed),
            pl.BlockSpec((1, HEAD_DIM), fixed),
            pl.BlockSpec(p.shape, fixed),
            pl.BlockSpec(const_rows.shape, fixed),
        ],
        out_specs=[
            pl.BlockSpec((None, SUBLANES, seq), lambda b: (b, 0, 0)),
            pl.BlockSpec((seq, LANES), lambda b: (b, 0)),
            pl.BlockSpec((seq, LANES), lambda b: (b, 0)),
        ],
        out_shape=[
            jax.ShapeDtypeStruct((batch, SUBLANES, seq), F32),
            jax.ShapeDtypeStruct((t, LANES), BF16),
            jax.ShapeDtypeStruct((t, LANES), BF16),
        ],
        compiler_params=_params(("parallel",)),
        name="fox_gate",
    )(small, fb, gq, gk, p, const_rows)


def _fox_kernel(q_ref, k_ref, v_ref, fg_ref, aq_ref, ak_ref, crow_ref, gq_ref, gk_ref, ong_ref,
                o_ref, ka_scr, v1_scr):
    s = q_ref.shape[0]
    h = pl.program_id(1)
    scale = HEAD_DIM ** -0.5 * LOG2E
    lane = lax.broadcasted_iota(jnp.int32, (FOX_TQ, LANES), 1)
    own = (lane >= FOX_AUG * h) & (lane < FOX_AUG * (h + 1))
    qi = lax.broadcasted_iota(jnp.int32, (FOX_TQ, FOX_TQ), 0)
    ki = lax.broadcasted_iota(jnp.int32, (FOX_TQ, FOX_TQ), 1)
    causal = qi >= ki
    ones = jnp.ones((FOX_TQ, HEAD_DIM), BF16)

    def prepare(rows):
        qn = (_rms(q_ref[rows, :].astype(F32), gq_ref[...]) * scale).astype(BF16)
        kn = _rms(k_ref[rows, :].astype(F32), gk_ref[...]).astype(BF16)
        qa = jnp.concatenate(
            [qn, jnp.where(own, aq_ref[rows, :].astype(F32), 0.0).astype(BF16)], axis=1)
        ka = jnp.concatenate([kn, ak_ref[rows, :]], axis=1)
        v1 = jnp.concatenate([v_ref[rows, :], ones], axis=1)
        ka_scr[rows, :] = ka
        v1_scr[rows, :] = v1
        return qa, ka, v1

    def finish(rows, ol):
        o = ol[:, :HEAD_DIM] / ol[:, HEAD_DIM:]
        on = _rms(o, ong_ref[...])
        o_ref[rows, :] = (on * _sigmoid(fg_ref[rows, :].astype(F32))).astype(BF16)

    bounded = 2.0 * _qk_bound(gq_ref[...], gk_ref[...])[0, 0] <= FOX_MAX_SPAN

    @pl.when(bounded)
    def _():
        def logits(qb):
            r0 = qb * FOX_TQ
            rows = slice(r0, r0 + FOX_TQ)
            qa, ka, v1 = prepare(rows)
            diag = jnp.where(causal, _dot_nt(qa, ka), NEG_BIG)
            past = _dot_nt(qa, ka_scr[:r0, :]) if r0 else None
            return rows, r0, diag, past, v1

        def weighted_values(rows, r0, diag, past, v1):
            ol = _dot(jnp.exp2(diag).astype(BF16), v1)
            if r0:
                ol = ol + _dot(jnp.exp2(past).astype(BF16), v1_scr[:r0, :])
            finish(rows, ol)

        n_blocks = s // FOX_TQ
        pending = logits(0)
        for qb in range(n_blocks):
            nxt = logits(qb + 1) if qb + 1 < n_blocks else None
            weighted_values(*pending)
            pending = nxt

    @pl.when(jnp.logical_not(bounded))
    def _():
        for qb in range(s // FOX_TQ):
            r0 = qb * FOX_TQ
            rows = slice(r0, r0 + FOX_TQ)
            qa, ka, v1 = prepare(rows)
            qn = qa[:, :HEAD_DIM]
            diag = _dot_nt(qn, ka[:, :HEAD_DIM]) - crow_ref[:, rows]
            diag = jnp.where(causal, diag, NEG_BIG)
            m = jnp.max(diag, axis=-1, keepdims=True)
            if r0:
                past = _dot_nt(qn, ka_scr[:r0, :HEAD_DIM]) - crow_ref[:, :r0]
                m = jnp.maximum(m, jnp.max(past, axis=-1, keepdims=True))
                ol = _dot(jnp.exp2(past - m).astype(BF16), v1_scr[:r0, :])
                ol = ol + _dot(jnp.exp2(diag - m).astype(BF16), v1)
            else:
                ol = _dot(jnp.exp2(diag - m).astype(BF16), v1)
            finish(rows, ol)


def _fox(main, aq, ak, crow4, gq, gk, ong, batch, seq):
    t = main.shape[0]
    cb = lambda off: off // HEAD_DIM
    return pl.pallas_call(
        _fox_kernel,
        grid=(batch, FOX_HEADS),
        in_specs=[
            pl.BlockSpec((seq, HEAD_DIM), lambda b, h: (b, cb(OFF_FQ) + h)),
            pl.BlockSpec((seq, HEAD_DIM), lambda b, h: (b, cb(OFF_FK) + h)),
            pl.BlockSpec((seq, HEAD_DIM), lambda b, h: (b, cb(OFF_FV) + h)),
            pl.BlockSpec((seq, HEAD_DIM), lambda b, h: (b, cb(OFF_FG) + h)),
            pl.BlockSpec((seq, LANES), lambda b, h: (b, 0)),
            pl.BlockSpec((seq, LANES), lambda b, h: (b, 0)),
            pl.BlockSpec((None, None, 1, seq), lambda b, h: (b, h, 0, 0)),
            pl.BlockSpec((1, HEAD_DIM), lambda b, h: (0, 0)),
            pl.BlockSpec((1, HEAD_DIM), lambda b, h: (0, 0)),
            pl.BlockSpec((None, 1, HEAD_DIM), lambda b, h: (h, 0, 0)),
        ],
        out_specs=pl.BlockSpec((seq, HEAD_DIM), lambda b, h: (b, h)),
        out_shape=jax.ShapeDtypeStruct((t, FOX_HEADS * HEAD_DIM), BF16),
        scratch_shapes=[
            pltpu.VMEM((seq, HEAD_DIM + LANES), BF16),
            pltpu.VMEM((seq, 2 * HEAD_DIM), BF16),
        ],
        compiler_params=_params(("parallel", "arbitrary")),
        name="fox",
    )(main, main, main, main, aq, ak, crow4, gq, gk, ong)


def _memkv_kernel(mem_ref, g_ref, w_ref, gq_ref, gk_ref, mk_ref, mv_ref):
    mn = _rms(mem_ref[...], g_ref[...]).astype(BF16)
    kv = _dot(mn, w_ref[...])
    n_mem = kv.shape[0]
    dk = MEM_HEADS * HEAD_DIM
    ones = jnp.ones((n_mem, HEAD_DIM), BF16)
    qk_bound = _qk_bound(gq_ref[...], gk_ref[...])
    b_hi = qk_bound.astype(BF16).astype(F32)
    b_lo = (qk_bound - b_hi).astype(BF16).astype(F32)
    lane = lax.broadcasted_iota(jnp.int32, (n_mem, LANES), 1)
    shift = jnp.where(lane == 0, -b_hi, jnp.where(lane == 1, -b_lo, 0.0)).astype(BF16)
    for hh in range(MEM_HEADS):
        hl = slice(hh * HEAD_DIM, (hh + 1) * HEAD_DIM)
        mk_ref[:, 2 * hh * HEAD_DIM:(2 * hh + 1) * HEAD_DIM] = _rms(kv[:, hl], gk_ref[...]).astype(BF16)
        mk_ref[:, (2 * hh + 1) * HEAD_DIM:(2 * hh + 2) * HEAD_DIM] = shift
        mv_ref[:, 2 * hh * HEAD_DIM:(2 * hh + 1) * HEAD_DIM] = kv[:, dk + hh * HEAD_DIM:dk + (hh + 1) * HEAD_DIM].astype(BF16)
        mv_ref[:, (2 * hh + 1) * HEAD_DIM:(2 * hh + 2) * HEAD_DIM] = ones


def _mem_kv(mem2, g, w, gq, gk, batch, n_mem):
    d = mem2.shape[1]
    dk = MEM_HEADS * HEAD_DIM
    return pl.pallas_call(
        _memkv_kernel,
        grid=(batch,),
        in_specs=[
            pl.BlockSpec((n_mem, d), lambda b: (b, 0)),
            pl.BlockSpec((1, d), lambda b: (0, 0)),
            pl.BlockSpec((d, 2 * dk), lambda b: (0, 0)),
            pl.BlockSpec((1, HEAD_DIM), lambda b: (0, 0)),
            pl.BlockSpec((1, HEAD_DIM), lambda b: (0, 0)),
        ],
        out_specs=[
            pl.BlockSpec((n_mem, 2 * dk), lambda b: (b, 0)),
            pl.BlockSpec((n_mem, 2 * dk), lambda b: (b, 0)),
        ],
        out_shape=[
            jax.ShapeDtypeStruct((batch * n_mem, 2 * dk), BF16),
            jax.ShapeDtypeStruct((batch * n_mem, 2 * dk), BF16),
        ],
        compiler_params=_params(("parallel",)),
        name="mem_kv",
    )(mem2, g, w, gq, gk)


def _memattn_kernel(q_ref, mg_ref, mk_ref, mv_ref, gq_ref, gk_ref, ong_ref, o_ref):
    s = q_ref.shape[0]
    scale = HEAD_DIM ** -0.5 * LOG2E
    lane = lax.broadcasted_iota(jnp.int32, (MEM_TQ, LANES), 1)
    q_extra = jnp.where(lane < 2, 1.0, 0.0).astype(BF16)

    def q_block(rows):
        return (_rms(q_ref[rows, :].astype(F32), gq_ref[...]) * scale).astype(BF16)

    def finish(rows, ol):
        o = ol[:, :HEAD_DIM] / ol[:, HEAD_DIM:]
        on = _rms(o, ong_ref[...])
        o_ref[rows, :] = (on * _sigmoid(mg_ref[rows, :].astype(F32))).astype(BF16)

    bounded = 2.0 * _qk_bound(gq_ref[...], gk_ref[...])[0, 0] <= FOX_MAX_SPAN

    @pl.when(bounded)
    def _():
        def logits(qb):
            rows = slice(qb * MEM_TQ, (qb + 1) * MEM_TQ)
            qa = jnp.concatenate([q_block(rows), q_extra], axis=1)
            return rows, _dot_nt(qa, mk_ref[...])

        n_blocks = s // MEM_TQ
        pending = logits(0)
        for qb in range(n_blocks):
            nxt = logits(qb + 1) if qb + 1 < n_blocks else None
            rows, lg = pending
            finish(rows, _dot(jnp.exp2(lg).astype(BF16), mv_ref[...]))
            pending = nxt

    @pl.when(jnp.logical_not(bounded))
    def _():
        for qb in range(s // MEM_TQ):
            rows = slice(qb * MEM_TQ, (qb + 1) * MEM_TQ)
            logits = _dot_nt(q_block(rows), mk_ref[:, :HEAD_DIM])
            m = jnp.max(logits, axis=-1, keepdims=True)
            finish(rows, _dot(jnp.exp2(logits - m).astype(BF16), mv_ref[...]))


def _mem_attn(main, mkn, mv, gq, gk, ong, batch, seq, n_mem):
    t = main.shape[0]
    cb = lambda off: off // HEAD_DIM
    return pl.pallas_call(
        _memattn_kernel,
        grid=(batch, MEM_HEADS),
        in_specs=[
            pl.BlockSpec((seq, HEAD_DIM), lambda b, h: (b, cb(OFF_MQ) + h)),
            pl.BlockSpec((seq, HEAD_DIM), lambda b, h: (b, cb(OFF_MG) + h)),
            pl.BlockSpec((n_mem, 2 * HEAD_DIM), lambda b, h: (b, h)),
            pl.BlockSpec((n_mem, 2 * HEAD_DIM), lambda b, h: (b, h)),
            pl.BlockSpec((1, HEAD_DIM), lambda b, h: (0, 0)),
            pl.BlockSpec((1, HEAD_DIM), lambda b, h: (0, 0)),
            pl.BlockSpec((None, 1, HEAD_DIM), lambda b, h: (h, 0, 0)),
        ],
        out_specs=pl.BlockSpec((seq, HEAD_DIM), lambda b, h: (b, h)),
        out_shape=jax.ShapeDtypeStruct((t, MEM_HEADS * HEAD_DIM), BF16),
        compiler_params=_params(("parallel", "arbitrary")),
        name="mem_attn",
    )(main, main, mkn, mv, gq, gk, ong)


def _outproj_kernel(og_ref, of_ref, om_ref, x_ref, wo_ref, gm_ref, h_ref, hn_ref):
    n_g = og_ref.shape[1]
    n_f = of_ref.shape[1]
    d = x_ref.shape[1]
    slab = d // OUT_PROJ_SLABS
    sumsq = jnp.zeros((x_ref.shape[0], 1), F32)
    for c in range(OUT_PROJ_SLABS):
        cols = slice(c * slab, (c + 1) * slab)
        acc = _dot(og_ref[...], wo_ref[:n_g, cols])
        acc = acc + _dot(of_ref[...], wo_ref[n_g:n_g + n_f, cols])
        acc = acc + _dot(om_ref[...], wo_ref[n_g + n_f:, cols])
        h = x_ref[:, cols] + acc
        h_ref[:, cols] = h
        sumsq = sumsq + jnp.sum(h * h, axis=-1, keepdims=True)
    inv = lax.rsqrt(sumsq * (1.0 / d) + EPS)
    hn_ref[...] = (h_ref[...] * inv * gm_ref[...]).astype(BF16)


def _out_proj(og, of, om, x2, wo, gm, tm=512):
    t, d = x2.shape
    row = lambda i: (i, 0)
    fixed = lambda i: (0, 0)
    return pl.pallas_call(
        _outproj_kernel,
        grid=(t // tm,),
        in_specs=[
            pl.BlockSpec((tm, og.shape[1]), row),
            pl.BlockSpec((tm, of.shape[1]), row),
            pl.BlockSpec((tm, om.shape[1]), row),
            pl.BlockSpec((tm, d), row),
            pl.BlockSpec(wo.shape, fixed),
            pl.BlockSpec((1, d), fixed),
        ],
        out_specs=[pl.BlockSpec((tm, d), row), pl.BlockSpec((tm, d), row)],
        out_shape=[jax.ShapeDtypeStruct((t, d), F32), jax.ShapeDtypeStruct((t, d), BF16)],
        compiler_params=_params(("parallel",)),
        name="out_proj",
    )(og, of, om, x2, wo, gm)


def _mlp_kernel(hn_ref, h_ref, wu_ref, wd_ref, o_ref):
    def hidden_contribution():
        tf = wu_ref.shape[1]
        piece = tf // MLP_SPLIT
        acc = None
        for s in range(MLP_SPLIT):
            cols = slice(s * piece, (s + 1) * piece)
            u = jnp.maximum(_dot(hn_ref[...], wu_ref[:, cols]), 0.0)
            part = _dot((u * u).astype(BF16), wd_ref[cols, :])
            acc = part if acc is None else acc + part
        return acc

    @pl.when(pl.program_id(1) == 0)
    def _():
        o_ref[...] = h_ref[...] + hidden_contribution()

    @pl.when(pl.program_id(1) > 0)
    def _():
        o_ref[...] += hidden_contribution()


def _mlp(hn, h, wu, wd, tm=512, tf=2048):
    t, d = h.shape
    dff = wu.shape[1]
    blocks = 2 * (2 * d * tf * 2 + tm * d * 2 + 2 * tm * d * 4)
    vmem_limit = blocks + MLP_VMEM_TEMPS
    return pl.pallas_call(
        _mlp_kernel,
        grid=(t // tm, dff // tf),
        in_specs=[
            pl.BlockSpec((tm, d), lambda i, f: (i, 0)),
            pl.BlockSpec((tm, d), lambda i, f: (i, 0)),
            pl.BlockSpec((d, tf), lambda i, f: (0, f)),
            pl.BlockSpec((tf, d), lambda i, f: (f, 0)),
        ],
        out_specs=pl.BlockSpec((tm, d), lambda i, f: (i, 0)),
        out_shape=jax.ShapeDtypeStruct((t, d), F32),
        compiler_params=_params(("parallel", "arbitrary"), vmem_limit),
        name="mlp",
    )(hn, h, wu, wd)


def _regroup_kernel(wt_ref, main_ref, small_ref):
    n_gla = OFF_FQ
    n_fox = OFF_MQ - OFF_FQ
    a0 = n_gla
    f0 = a0 + GLA_RANK
    ff0 = f0 + n_fox
    m0 = ff0 + FOX_HEADS
    n_gq = GLA_HEADS * GLA_DK
    q_scale = GLA_DK ** -0.5
    assert math.frexp(q_scale)[0] == 0.5
    main_ref[:n_gq, :] = (wt_ref[:n_gq, :] * q_scale).astype(BF16)
    main_ref[n_gq:n_gla, :] = wt_ref[n_gq:n_gla, :].astype(BF16)
    main_ref[OFF_FQ:OFF_MQ, :] = wt_ref[f0:ff0, :].astype(BF16)
    main_ref[OFF_MQ:, :] = wt_ref[m0:, :].astype(BF16)
    pad = jnp.zeros((LANES - GLA_RANK - FOX_HEADS, wt_ref.shape[1]), F32)
    small_ref[...] = jnp.concatenate(
        [wt_ref[a0:f0, :], wt_ref[ff0:m0, :], pad], axis=0).astype(BF16)


def _regroup_w_in(w_in, tc=256):
    wt = w_in.T
    n_in, d = wt.shape
    return pl.pallas_call(
        _regroup_kernel,
        grid=(d // tc,),
        in_specs=[pl.BlockSpec((n_in, tc), lambda i: (0, i))],
        out_specs=[pl.BlockSpec((N_MAIN, tc), lambda i: (0, i)),
                   pl.BlockSpec((LANES, tc), lambda i: (0, i))],
        out_shape=[jax.ShapeDtypeStruct((N_MAIN, d), BF16), jax.ShapeDtypeStruct((LANES, d), BF16)],
        compiler_params=_params(("parallel",)),
        name="regroup_w_in",
    )(wt)


def kernel(x, mem, attn_norm_g, w_in, gla_a_w2, gla_a_b, fox_f_b, fox_q_norm_g, fox_k_norm_g,
           mem_norm_g, w_mem_kv, mem_q_norm_g, mem_k_norm_g, out_norm_g, w_out, mlp_norm_g,
           w_up, w_down):
    batch, seq, d = x.shape
    n_mem = mem.shape[1]
    t = batch * seq
    x2 = x.reshape(t, d)
    row = lambda a: a.reshape(1, -1).astype(F32)

    w_main, w_small = _regroup_w_in(w_in)
    main, small, w_mem16, w_out16, w_up16, w_down16 = _in_proj(
        x2, row(attn_norm_g), w_main, w_small, (w_mem_kv, w_out, w_up, w_down))

    n_gla = GLA_HEADS * HEAD_DIM
    n_fox = FOX_HEADS * HEAD_DIM
    fb = jnp.zeros((SUBLANES, 1), F32).at[:FOX_HEADS, 0].set(fox_f_b)
    crow, aq, ak = _fox_gate(small, fb, row(fox_q_norm_g), row(fox_k_norm_g), batch, seq)
    ong_fox = out_norm_g[n_gla:n_gla + n_fox].reshape(FOX_HEADS, 1, HEAD_DIM)
    of = _fox(main, aq, ak, crow.reshape(batch, SUBLANES, 1, seq), row(fox_q_norm_g),
              row(fox_k_norm_g), ong_fox, batch, seq)

    mkn, mv = _mem_kv(mem.reshape(batch * n_mem, d), row(mem_norm_g), w_mem16,
                      row(mem_q_norm_g), row(mem_k_norm_g), batch, n_mem)
    ong_mem = out_norm_g[n_gla + n_fox:].reshape(MEM_HEADS, 1, HEAD_DIM)
    om = _mem_attn(main, mkn, mv, row(mem_q_norm_g), row(mem_k_norm_g), ong_mem, batch, seq, n_mem)

    w2p = jnp.zeros((LANES, GLA_HEADS * GLA_DK), F32).at[:GLA_RANK].set(gla_a_w2).astype(BF16)
    og = _gla(main, small, w2p, row(gla_a_b), row(out_norm_g[:n_gla]), batch, seq)

    h, hn = _out_proj(og, of, om, x2, w_out16, row(mlp_norm_g))
    y = _mlp(hn, h, w_up16, w_down16)
    return y.reshape(batch, seq, d)
```

```python
import functools
import math

import jax
import jax.numpy as jnp
import numpy as np
from jax import lax
from jax.experimental import pallas as pl
from jax.experimental.pallas import tpu as pltpu

F32 = jnp.float32
BF16 = jnp.bfloat16

HEAD_DIM = 128
GLA_HEADS = 8
GLA_DK = 64
GLA_RANK = 16
GLA_TAU = 16.0
FOX_HEADS = 4
MEM_HEADS = 4
EPS = 1e-6

LANES = 128
SUBLANES = 8
LOG2E = 1.4426950408889634
GLA_CHUNK = 64
GLA_STEP = 1024
GLA_CUMSUM_BLOCK = 256
GLA_SAFE_EXP = 86.0
FOX_TQ = 256
FOX_AUG = 8
FOX_MAX_SPAN = 100.0
MEM_TQ = 512
OUT_PROJ_SLABS = 4
NEG_BIG = -1e30
VMEM_LIMIT = 50 * 1024 * 1024
MLP_VMEM_TEMPS = 5 * 1024 * 1024
MLP_SPLIT = 2
INPROJ_VMEM_TEMPS = 8 * 1024 * 1024

OFF_GQ, OFF_GK, OFF_GV, OFF_GG = 0, 512, 1024, 2048
OFF_FQ, OFF_FK, OFF_FV, OFF_FG = 3072, 3584, 4096, 4608
OFF_MQ, OFF_MG = 5120, 5632
N_MAIN = 6144
SMALL_FF = GLA_RANK
FF_ROW0 = SMALL_FF
assert FF_ROW0 % SUBLANES == 0 and FOX_HEADS <= SUBLANES


def _dot(a, b):
    return jnp.dot(a, b, preferred_element_type=F32)


def _dot_nt(a, b):
    return lax.dot_general(a, b, (((1,), (1,)), ((), ())), preferred_element_type=F32)


def _dot_tn(a, b):
    return lax.dot_general(a, b, (((0,), (0,)), ((), ())), preferred_element_type=F32)


def _rms(x, g):
    return x * lax.rsqrt(jnp.mean(x * x, axis=-1, keepdims=True) + EPS) * g


def _log_sigmoid(z):
    return jnp.minimum(z, 0.0) - jnp.log(1.0 + jnp.exp(-jnp.abs(z)))


def _sigmoid(z):
    return 1.0 / (1.0 + jnp.exp(-z))


def _split3(a):
    hi = a.astype(BF16)
    r = a - hi.astype(F32)
    mid = r.astype(BF16)
    lo = (r - mid.astype(F32)).astype(BF16)
    return hi, mid, lo


def _params(sem, vmem_limit=VMEM_LIMIT):
    return pltpu.CompilerParams(dimension_semantics=sem, vmem_limit_bytes=vmem_limit)


def _inproj_kernel(*refs, n_riders):
    x_ref, g_ref, wm_ref, ws_ref = refs[:4]
    rider_in = refs[4:4 + n_riders]
    main_ref, small_ref = refs[4 + n_riders:6 + n_riders]
    rider_out = refs[6 + n_riders:6 + 2 * n_riders]
    xn_even, xn_odd = refs[6 + 2 * n_riders:]
    i = pl.program_id(0)
    j = pl.program_id(1)
    n_col = pl.num_programs(1)
    rs = x_ref.shape[0] // n_col
    odd = i % 2 == 1

    def normalise_slice(dst):
        rows = pl.ds(pl.multiple_of(j * rs, rs), rs)
        dst[rows, :] = _rms(x_ref[rows, :], g_ref[...]).astype(BF16)

    def step(src, dst):
        @pl.when(j == 0)
        def _():
            small_ref[...] = _dot_nt(src[...], ws_ref[...])

        normalise_slice(dst)
        for w32, w16 in zip(rider_in, rider_out):
            w16[...] = w32[...].astype(BF16)
        main_ref[...] = _dot_nt(src[...], wm_ref[...]).astype(BF16)

    @pl.when(i == 0)
    def _():
        normalise_slice(xn_even)

    @pl.when(odd)
    def _():
        step(xn_even, xn_odd)

    @pl.when((i > 0) & jnp.logical_not(odd))
    def _():
        step(xn_odd, xn_even)


def _in_proj(x2, g, w_main, w_small, riders, tm=1024, tn=1536):
    t, d = x2.shape
    n_row = t // tm
    n_col = N_MAIN // tn
    n_steps = n_row * n_col
    prev = lambda i: jnp.maximum(i - 1, 0)
    slab = lambda i, j: (jnp.where(i > 0, (i - 1) * n_col + j, 0), 0)
    rider_specs = []
    rider_bytes = 0
    for w in riders:
        assert w.shape[0] % (n_steps * 2 * SUBLANES) == 0
        rider_specs.append(pl.BlockSpec((w.shape[0] // n_steps, w.shape[1]), slab))
        rider_bytes += w.size // n_steps * (4 + 2)
    blocks = 2 * (tm * d * 4 + tn * d * 2 + LANES * d * 2 + tm * tn * 2 + tm * LANES * 4 + rider_bytes)
    vmem_limit = blocks + 2 * tm * d * 2 + INPROJ_VMEM_TEMPS
    return pl.pallas_call(
        functools.partial(_inproj_kernel, n_riders=len(riders)),
        grid=(n_row + 1, n_col),
        in_specs=[
            pl.BlockSpec((tm, d), lambda i, j: (jnp.minimum(i, n_row - 1), 0)),
            pl.BlockSpec((1, d), lambda i, j: (0, 0)),
            pl.BlockSpec((tn, d), lambda i, j: (j, 0)),
            pl.BlockSpec((LANES, d), lambda i, j: (0, 0)),
        ] + rider_specs,
        out_specs=[
            pl.BlockSpec((tm, tn), lambda i, j: (prev(i), jnp.where(i > 0, j, 0))),
            pl.BlockSpec((tm, LANES), lambda i, j: (prev(i), 0)),
        ] + rider_specs,
        out_shape=[
            jax.ShapeDtypeStruct((t, N_MAIN), BF16),
            jax.ShapeDtypeStruct((t, LANES), F32),
        ] + [jax.ShapeDtypeStruct(w.shape, BF16) for w in riders],
        scratch_shapes=[pltpu.VMEM((tm, d), BF16), pltpu.VMEM((tm, d), BF16)],
        compiler_params=_params(("arbitrary", "arbitrary"), vmem_limit),
        name="in_proj",
    )(x2, g, w_main, w_small, *riders)


def _gla_kernel(q_ref, k_ref, v_ref, gg_ref, small_ref, small_next_ref, w2_ref, ab_ref, ong_ref,
                o_ref, st_ref, bc_scr, safe_ref, qs_scr, bs_scr, os_scr):
    C = GLA_CHUNK
    n_pairs = GLA_HEADS // 2
    step = pl.program_id(0) * pl.num_programs(1) + pl.program_id(1)
    slot = step % 2

    @pl.when(pl.program_id(1) == 0)
    def _():
        st_ref[...] = jnp.zeros_like(st_ref)

    def decay_sums(src_ref, dst):
        z = _dot(src_ref[...].astype(BF16), w2_ref[...]) + ab_ref[...]
        la = _log_sigmoid(z) * (LOG2E / GLA_TAU)
        blk = GLA_CUMSUM_BLOCK
        ri = lax.broadcasted_iota(jnp.int32, (blk, blk), 0)
        ci = lax.broadcasted_iota(jnp.int32, (blk, blk), 1)
        same_chunk = (ri ^ ci) < C
        tri = jnp.where(same_chunk, jnp.where(ci <= ri, 1.0, 0.0), 0.0).astype(BF16)
        for i in range(GLA_STEP // blk):
            rows = slice(i * blk, (i + 1) * blk)
            hi, mid, lo = _split3(la[rows, :])
            bc_scr[dst, rows, :] = _dot(tri, hi) + _dot(tri, mid) + _dot(tri, lo)
        safe_ref[dst] = (jnp.min(la) * C >= -GLA_SAFE_EXP).astype(jnp.int32)

    @pl.when(step == 0)
    def _():
        decay_sums(small_ref, 0)

    lane = lax.broadcasted_iota(jnp.int32, (C, LANES), 1)
    low_half = lane < GLA_DK
    st_r = lax.broadcasted_iota(jnp.int32, (2 * HEAD_DIM, LANES), 0)
    st_c = lax.broadcasted_iota(jnp.int32, (2 * HEAD_DIM, LANES), 1)
    st_mask = (st_r < HEAD_DIM) == (st_c < GLA_DK)
    a_r = lax.broadcasted_iota(jnp.int32, (C, 2 * C), 0)
    a_c = lax.broadcasted_iota(jnp.int32, (C, 2 * C), 1)
    causal = a_r >= (a_c & (C - 1))
    zeros_v = jnp.zeros((C, HEAD_DIM), BF16)
    jrow = lax.broadcasted_iota(jnp.int32, (C, LANES), 0)

    def unit_first_half(ch, p, fast):
        rows = slice(ch * C, (ch + 1) * C)
        kl = slice(p * LANES, (p + 1) * LANES)
        vl = slice(p * 2 * HEAD_DIM, (p + 1) * 2 * HEAD_DIM)
        b = bc_scr[slot, rows, kl]
        qs = q_ref[rows, kl].astype(F32)
        kf = k_ref[rows, kl].astype(F32)
        v2 = v_ref[rows, vl]
        qe16 = (qs * jnp.exp2(b)).astype(BF16)
        b_last = b[C - 1:C, :]
        st = st_ref[p]
        o = _dot_nt(qe16, st.astype(BF16))
        if fast:
            ke = kf * jnp.exp2(-b)
            kbd = jnp.concatenate(
                [jnp.where(low_half, ke, 0.0), jnp.where(low_half, 0.0, ke)], axis=0).astype(BF16)
            scores = _dot_nt(qe16, kbd)
            vbd = jnp.concatenate(
                [jnp.concatenate([v2[:, :HEAD_DIM], zeros_v], axis=1),
                 jnp.concatenate([zeros_v, v2[:, HEAD_DIM:]], axis=1)], axis=0)
            upd = _dot_tn(vbd, kbd)
            st_ref[p] = jnp.exp2(b_last) * (st + upd)
            return rows, p, o, scores, vbd
        else:
            kd16 = (kf * jnp.exp2(b_last - b)).astype(BF16)
            upd = _dot_tn(v2, kd16)
            st_ref[p] = jnp.exp2(b_last) * st + jnp.where(st_mask, upd, 0.0)
            qs_scr[...] = qs
            bs_scr[...] = b
            v2f = v2.astype(F32)

            def row(i, carry):
                qi = qs_scr[pl.ds(i, 1), :]
                bi = bs_scr[pl.ds(i, 1), :]
                w = qi * kf * jnp.exp2(jnp.minimum(bi - b, 0.0))
                w = jnp.where(jrow <= i, w, 0.0)
                p0 = jnp.sum(jnp.where(low_half, w, 0.0), axis=1, keepdims=True)
                p1 = jnp.sum(jnp.where(low_half, 0.0, w), axis=1, keepdims=True)
                pv = jnp.concatenate([p0 * v2f[:, :HEAD_DIM], p1 * v2f[:, HEAD_DIM:]], axis=1)
                os_scr[pl.ds(i, 1), :] = jnp.sum(pv, axis=0, keepdims=True)
                return carry

            lax.fori_loop(0, C, row, 0)
            return rows, p, o + os_scr[...], None, None

    def unit_second_half(rows, p, o, scores, vbd):
        if scores is not None:
            o = o + _dot(jnp.where(causal, scores, 0.0).astype(BF16), vbd)
        for hh in range(2):
            hl = slice((2 * p + hh) * HEAD_DIM, (2 * p + hh + 1) * HEAD_DIM)
            on = _rms(o[:, hh * HEAD_DIM:(hh + 1) * HEAD_DIM], ong_ref[:, hl])
            g = gg_ref[rows, hl].astype(F32)
            o_ref[rows, hl] = (on * (g * _sigmoid(g))).astype(BF16)

    def run(fast):
        n_chunks = GLA_STEP // C
        for ch in range(n_chunks):
            if ch == n_chunks // 2:
                decay_sums(small_next_ref, 1 - slot)
            for p in range(n_pairs):
                unit_second_half(*unit_first_half(ch, p, fast))

    safe = safe_ref[slot]

    @pl.when(safe == 1)
    def _():
        run(True)

    @pl.when(safe == 0)
    def _():
        run(False)


def _gla(main, small, w2p, ab, ong, batch, seq):
    t = main.shape[0]
    nc = seq // GLA_STEP
    rb = lambda b, c: b * nc + c
    rb_next = lambda b, c: jnp.minimum(b * nc + c + 1, batch * nc - 1)
    dq = GLA_HEADS * GLA_DK
    dv = GLA_HEADS * HEAD_DIM
    return pl.pallas_call(
        _gla_kernel,
        grid=(batch, nc),
        in_specs=[
            pl.BlockSpec((GLA_STEP, dq), lambda b, c: (rb(b, c), OFF_GQ // dq)),
            pl.BlockSpec((GLA_STEP, dq), lambda b, c: (rb(b, c), OFF_GK // dq)),
            pl.BlockSpec((GLA_STEP, dv), lambda b, c: (rb(b, c), OFF_GV // dv)),
            pl.BlockSpec((GLA_STEP, dv), lambda b, c: (rb(b, c), OFF_GG // dv)),
            pl.BlockSpec((GLA_STEP, LANES), lambda b, c: (rb(b, c), 0)),
            pl.BlockSpec((GLA_STEP, LANES), lambda b, c: (rb_next(b, c), 0)),
            pl.BlockSpec((LANES, dq), lambda b, c: (0, 0)),
            pl.BlockSpec((1, dq), lambda b, c: (0, 0)),
            pl.BlockSpec((1, dv), lambda b, c: (0, 0)),
        ],
        out_specs=pl.BlockSpec((GLA_STEP, dv), lambda b, c: (rb(b, c), 0)),
        out_shape=jax.ShapeDtypeStruct((t, dv), BF16),
        scratch_shapes=[
            pltpu.VMEM((GLA_HEADS // 2, 2 * HEAD_DIM, LANES), F32),
            pltpu.VMEM((2, GLA_STEP, dq), F32),
            pltpu.SMEM((2,), jnp.int32),
            pltpu.VMEM((GLA_CHUNK, LANES), F32),
            pltpu.VMEM((GLA_CHUNK, LANES), F32),
            pltpu.VMEM((GLA_CHUNK, 2 * HEAD_DIM), F32),
        ],
        compiler_params=_params(("arbitrary", "arbitrary")),
        name="gla",
    )(main, main, main, main, small, small, w2p, ab, ong)


def _foxgate_kernel(small_ref, fb_ref, gq_ref, gk_ref, p_ref, rows_ref,
                    crow_ref, aq_ref, ak_ref):
    s = small_ref.shape[0]
    blk = 256
    ff = small_ref[...].T[FF_ROW0:FF_ROW0 + SUBLANES, :]
    lf = _log_sigmoid(ff + fb_ref[...]) * LOG2E
    ri = lax.broadcasted_iota(jnp.int32, (blk, blk), 0)
    ci = lax.broadcasted_iota(jnp.int32, (blk, blk), 1)
    tri = jnp.where(ri <= ci, 1.0, 0.0).astype(BF16)
    carry = jnp.zeros((SUBLANES, 1), F32)
    for i in range(s // blk):
        cols = slice(i * blk, (i + 1) * blk)
        hi, mid, lo = _split3(lf[:, cols])
        c = _dot(hi, tri) + _dot(mid, tri) + _dot(lo, tri) + carry
        crow_ref[:, cols] = c
        carry = c[:, blk - 1:blk]
    qk_bound = _qk_bound(gq_ref[...], gk_ref[...])
    b_hi = qk_bound.astype(BF16).astype(F32)
    b_lo = (qk_bound - b_hi).astype(BF16).astype(F32)
    q_const = rows_ref[0:1, :]
    k_const = rows_ref[1:2, :] + b_hi * rows_ref[2:3, :] + b_lo * rows_ref[3:4, :]
    hi, mid, lo = _split3(crow_ref[...])
    terms = jnp.concatenate([hi.astype(F32), mid.astype(F32), lo.astype(F32),
                             jnp.zeros((SUBLANES, s), F32)], axis=0).astype(BF16)
    aug = _dot_tn(terms, p_ref[...])
    aq_ref[...] = (aug[:, :LANES] + q_const).astype(BF16)
    ak_ref[...] = (aug[:, LANES:] + k_const).astype(BF16)


def _qk_bound(gq, gk):
    gmax = jnp.max(jnp.abs(gq), axis=-1, keepdims=True) * jnp.max(jnp.abs(gk), axis=-1, keepdims=True)
    return gmax * (1.01 * LOG2E * HEAD_DIM ** 0.5)


def _fox_aug_constants():
    p = np.zeros((4 * SUBLANES, 2 * LANES), np.float32)
    rows = np.zeros((SUBLANES, LANES), np.float32)
    for h in range(FOX_HEADS):
        base = FOX_AUG * h
        for j in range(3):
            p[j * SUBLANES + h, base + j] = 1.0
            p[j * SUBLANES + h, LANES + base + 3 + j] = -1.0
            rows[1, base + j] = 1.0
        rows[0, base + 3:base + 8] = 1.0
        rows[2, base + 6] = -1.0
        rows[3, base + 7] = -1.0
    return jnp.asarray(p, BF16), jnp.asarray(rows)


def _fox_gate(small, fb, gq, gk, batch, seq):
    t = small.shape[0]
    p, const_rows = _fox_aug_constants()
    fixed = lambda b: (0, 0)
    return pl.pallas_call(
        _foxgate_kernel,
        grid=(batch,),
        in_specs=[
            pl.BlockSpec((seq, LANES), lambda b: (b, 0)),
            pl.BlockSpec((SUBLANES, 1), fixed),
            pl.BlockSpec((1, HEAD_DIM), fixed),
            pl.BlockSpec((1, HEAD_DIM), fixed),
            pl.BlockSpec(p.shape, fixed),
            pl.BlockSpec(const_rows.shape, fixed),
        ],
        out_specs=[
            pl.BlockSpec((None, SUBLANES, seq), lambda b: (b, 0, 0)),
            pl.BlockSpec((seq, LANES), lambda b: (b, 0)),
            pl.BlockSpec((seq, LANES), lambda b: (b, 0)),
        ],
        out_shape=[
            jax.ShapeDtypeStruct((batch, SUBLANES, seq), F32),
            jax.ShapeDtypeStruct((t, LANES), BF16),
            jax.ShapeDtypeStruct((t, LANES), BF16),
        ],
        compiler_params=_params(("parallel",)),
        name="fox_gate",
    )(small, fb, gq, gk, p, const_rows)


def _fox_kernel(q_ref, k_ref, v_ref, fg_ref, aq_ref, ak_ref, crow_ref, gq_ref, gk_ref, ong_ref,
                o_ref, ka_scr, v1_scr):
    s = q_ref.shape[0]
    h = pl.program_id(1)
    scale = HEAD_DIM ** -0.5 * LOG2E
    lane = lax.broadcasted_iota(jnp.int32, (FOX_TQ, LANES), 1)
    own = (lane >= FOX_AUG * h) & (lane < FOX_AUG * (h + 1))
    qi = lax.broadcasted_iota(jnp.int32, (FOX_TQ, FOX_TQ), 0)
    ki = lax.broadcasted_iota(jnp.int32, (FOX_TQ, FOX_TQ), 1)
    causal = qi >= ki
    ones = jnp.ones((FOX_TQ, HEAD_DIM), BF16)

    def prepare(rows):
        qn = (_rms(q_ref[rows, :].astype(F32), gq_ref[...]) * scale).astype(BF16)
        kn = _rms(k_ref[rows, :].astype(F32), gk_ref[...]).astype(BF16)
        qa = jnp.concatenate(
            [qn, jnp.where(own, aq_ref[rows, :].astype(F32), 0.0).astype(BF16)], axis=1)
        ka = jnp.concatenate([kn, ak_ref[rows, :]], axis=1)
        v1 = jnp.concatenate([v_ref[rows, :], ones], axis=1)
        ka_scr[rows, :] = ka
        v1_scr[rows, :] = v1
        return qa, ka, v1

    def finish(rows, ol):
        o = ol[:, :HEAD_DIM] / ol[:, HEAD_DIM:]
        on = _rms(o, ong_ref[...])
        o_ref[rows, :] = (on * _sigmoid(fg_ref[rows, :].astype(F32))).astype(BF16)

    bounded = 2.0 * _qk_bound(gq_ref[...], gk_ref[...])[0, 0] <= FOX_MAX_SPAN

    @pl.when(bounded)
    def _():
        def logits(qb):
            r0 = qb * FOX_TQ
            rows = slice(r0, r0 + FOX_TQ)
            qa, ka, v1 = prepare(rows)
            diag = jnp.where(causal, _dot_nt(qa, ka), NEG_BIG)
            past = _dot_nt(qa, ka_scr[:r0, :]) if r0 else None
            return rows, r0, diag, past, v1

        def weighted_values(rows, r0, diag, past, v1):
            ol = _dot(jnp.exp2(diag).astype(BF16), v1)
            if r0:
                ol = ol + _dot(jnp.exp2(past).astype(BF16), v1_scr[:r0, :])
            finish(rows, ol)

        n_blocks = s // FOX_TQ
        pending = logits(0)
        for qb in range(n_blocks):
            nxt = logits(qb + 1) if qb + 1 < n_blocks else None
            weighted_values(*pending)
            pending = nxt

    @pl.when(jnp.logical_not(bounded))
    def _():
        for qb in range(s // FOX_TQ):
            r0 = qb * FOX_TQ
            rows = slice(r0, r0 + FOX_TQ)
            qa, ka, v1 = prepare(rows)
            qn = qa[:, :HEAD_DIM]
            diag = _dot_nt(qn, ka[:, :HEAD_DIM]) - crow_ref[:, rows]
            diag = jnp.where(causal, diag, NEG_BIG)
            m = jnp.max(diag, axis=-1, keepdims=True)
            if r0:
                past = _dot_nt(qn, ka_scr[:r0, :HEAD_DIM]) - crow_ref[:, :r0]
                m = jnp.maximum(m, jnp.max(past, axis=-1, keepdims=True))
                ol = _dot(jnp.exp2(past - m).astype(BF16), v1_scr[:r0, :])
                ol = ol + _dot(jnp.exp2(diag - m).astype(BF16), v1)
            else:
                ol = _dot(jnp.exp2(diag - m).astype(BF16), v1)
            finish(rows, ol)


def _fox(main, aq, ak, crow4, gq, gk, ong, batch, seq):
    t = main.shape[0]
    cb = lambda off: off // HEAD_DIM
    return pl.pallas_call(
        _fox_kernel,
        grid=(batch, FOX_HEADS),
        in_specs=[
            pl.BlockSpec((seq, HEAD_DIM), lambda b, h: (b, cb(OFF_FQ) + h)),
            pl.BlockSpec((seq, HEAD_DIM), lambda b, h: (b, cb(OFF_FK) + h)),
            pl.BlockSpec((seq, HEAD_DIM), lambda b, h: (b, cb(OFF_FV) + h)),
            pl.BlockSpec((seq, HEAD_DIM), lambda b, h: (b, cb(OFF_FG) + h)),
            pl.BlockSpec((seq, LANES), lambda b, h: (b, 0)),
            pl.BlockSpec((seq, LANES), lambda b, h: (b, 0)),
            pl.BlockSpec((None, None, 1, seq), lambda b, h: (b, h, 0, 0)),
            pl.BlockSpec((1, HEAD_DIM), lambda b, h: (0, 0)),
            pl.BlockSpec((1, HEAD_DIM), lambda b, h: (0, 0)),
            pl.BlockSpec((None, 1, HEAD_DIM), lambda b, h: (h, 0, 0)),
        ],
        out_specs=pl.BlockSpec((seq, HEAD_DIM), lambda b, h: (b, h)),
        out_shape=jax.ShapeDtypeStruct((t, FOX_HEADS * HEAD_DIM), BF16),
        scratch_shapes=[
            pltpu.VMEM((seq, HEAD_DIM + LANES), BF16),
            pltpu.VMEM((seq, 2 * HEAD_DIM), BF16),
        ],
        compiler_params=_params(("parallel", "arbitrary")),
        name="fox",
    )(main, main, main, main, aq, ak, crow4, gq, gk, ong)


def _memkv_kernel(mem_ref, g_ref, w_ref, gq_ref, gk_ref, mk_ref, mv_ref):
    mn = _rms(mem_ref[...], g_ref[...]).astype(BF16)
    kv = _dot(mn, w_ref[...])
    n_mem = kv.shape[0]
    dk = MEM_HEADS * HEAD_DIM
    ones = jnp.ones((n_mem, HEAD_DIM), BF16)
    qk_bound = _qk_bound(gq_ref[...], gk_ref[...])
    b_hi = qk_bound.astype(BF16).astype(F32)
    b_lo = (qk_bound - b_hi).astype(BF16).astype(F32)
    lane = lax.broadcasted_iota(jnp.int32, (n_mem, LANES), 1)
    shift = jnp.where(lane == 0, -b_hi, jnp.where(lane == 1, -b_lo, 0.0)).astype(BF16)
    for hh in range(MEM_HEADS):
        hl = slice(hh * HEAD_DIM, (hh + 1) * HEAD_DIM)
        mk_ref[:, 2 * hh * HEAD_DIM:(2 * hh + 1) * HEAD_DIM] = _rms(kv[:, hl], gk_ref[...]).astype(BF16)
        mk_ref[:, (2 * hh + 1) * HEAD_DIM:(2 * hh + 2) * HEAD_DIM] = shift
        mv_ref[:, 2 * hh * HEAD_DIM:(2 * hh + 1) * HEAD_DIM] = kv[:, dk + hh * HEAD_DIM:dk + (hh + 1) * HEAD_DIM].astype(BF16)
        mv_ref[:, (2 * hh + 1) * HEAD_DIM:(2 * hh + 2) * HEAD_DIM] = ones


def _mem_kv(mem2, g, w, gq, gk, batch, n_mem):
    d = mem2.shape[1]
    dk = MEM_HEADS * HEAD_DIM
    return pl.pallas_call(
        _memkv_kernel,
        grid=(batch,),
        in_specs=[
            pl.BlockSpec((n_mem, d), lambda b: (b, 0)),
            pl.BlockSpec((1, d), lambda b: (0, 0)),
            pl.BlockSpec((d, 2 * dk), lambda b: (0, 0)),
            pl.BlockSpec((1, HEAD_DIM), lambda b: (0, 0)),
            pl.BlockSpec((1, HEAD_DIM), lambda b: (0, 0)),
        ],
        out_specs=[
            pl.BlockSpec((n_mem, 2 * dk), lambda b: (b, 0)),
            pl.BlockSpec((n_mem, 2 * dk), lambda b: (b, 0)),
        ],
        out_shape=[
            jax.ShapeDtypeStruct((batch * n_mem, 2 * dk), BF16),
            jax.ShapeDtypeStruct((batch * n_mem, 2 * dk), BF16),
        ],
        compiler_params=_params(("parallel",)),
        name="mem_kv",
    )(mem2, g, w, gq, gk)


def _memattn_kernel(q_ref, mg_ref, mk_ref, mv_ref, gq_ref, gk_ref, ong_ref, o_ref):
    s = q_ref.shape[0]
    scale = HEAD_DIM ** -0.5 * LOG2E
    lane = lax.broadcasted_iota(jnp.int32, (MEM_TQ, LANES), 1)
    q_extra = jnp.where(lane < 2, 1.0, 0.0).astype(BF16)

    def q_block(rows):
        return (_rms(q_ref[rows, :].astype(F32), gq_ref[...]) * scale).astype(BF16)

    def finish(rows, ol):
        o = ol[:, :HEAD_DIM] / ol[:, HEAD_DIM:]
        on = _rms(o, ong_ref[...])
        o_ref[rows, :] = (on * _sigmoid(mg_ref[rows, :].astype(F32))).astype(BF16)

    bounded = 2.0 * _qk_bound(gq_ref[...], gk_ref[...])[0, 0] <= FOX_MAX_SPAN

    @pl.when(bounded)
    def _():
        def logits(qb):
            rows = slice(qb * MEM_TQ, (qb + 1) * MEM_TQ)
            qa = jnp.concatenate([q_block(rows), q_extra], axis=1)
            return rows, _dot_nt(qa, mk_ref[...])

        n_blocks = s // MEM_TQ
        pending = logits(0)
        for qb in range(n_blocks):
            nxt = logits(qb + 1) if qb + 1 < n_blocks else None
            rows, lg = pending
            finish(rows, _dot(jnp.exp2(lg).astype(BF16), mv_ref[...]))
            pending = nxt

    @pl.when(jnp.logical_not(bounded))
    def _():
        for qb in range(s // MEM_TQ):
            rows = slice(qb * MEM_TQ, (qb + 1) * MEM_TQ)
            logits = _dot_nt(q_block(rows), mk_ref[:, :HEAD_DIM])
            m = jnp.max(logits, axis=-1, keepdims=True)
            finish(rows, _dot(jnp.exp2(logits - m).astype(BF16), mv_ref[...]))


def _mem_attn(main, mkn, mv, gq, gk, ong, batch, seq, n_mem):
    t = main.shape[0]
    cb = lambda off: off // HEAD_DIM
    return pl.pallas_call(
        _memattn_kernel,
        grid=(batch, MEM_HEADS),
        in_specs=[
            pl.BlockSpec((seq, HEAD_DIM), lambda b, h: (b, cb(OFF_MQ) + h)),
            pl.BlockSpec((seq, HEAD_DIM), lambda b, h: (b, cb(OFF_MG) + h)),
            pl.BlockSpec((n_mem, 2 * HEAD_DIM), lambda b, h: (b, h)),
            pl.BlockSpec((n_mem, 2 * HEAD_DIM), lambda b, h: (b, h)),
            pl.BlockSpec((1, HEAD_DIM), lambda b, h: (0, 0)),
            pl.BlockSpec((1, HEAD_DIM), lambda b, h: (0, 0)),
            pl.BlockSpec((None, 1, HEAD_DIM), lambda b, h: (h, 0, 0)),
        ],
        out_specs=pl.BlockSpec((seq, HEAD_DIM), lambda b, h: (b, h)),
        out_shape=jax.ShapeDtypeStruct((t, MEM_HEADS * HEAD_DIM), BF16),
        compiler_params=_params(("parallel", "arbitrary")),
        name="mem_attn",
    )(main, main, mkn, mv, gq, gk, ong)


def _outproj_kernel(og_ref, of_ref, om_ref, x_ref, wo_ref, gm_ref, h_ref, hn_ref):
    n_g = og_ref.shape[1]
    n_f = of_ref.shape[1]
    d = x_ref.shape[1]
    slab = d // OUT_PROJ_SLABS
    sumsq = jnp.zeros((x_ref.shape[0], 1), F32)
    for c in range(OUT_PROJ_SLABS):
        cols = slice(c * slab, (c + 1) * slab)
        acc = _dot(og_ref[...], wo_ref[:n_g, cols])
        acc = acc + _dot(of_ref[...], wo_ref[n_g:n_g + n_f, cols])
        acc = acc + _dot(om_ref[...], wo_ref[n_g + n_f:, cols])
        h = x_ref[:, cols] + acc
        h_ref[:, cols] = h
        sumsq = sumsq + jnp.sum(h * h, axis=-1, keepdims=True)
    inv = lax.rsqrt(sumsq * (1.0 / d) + EPS)
    hn_ref[...] = (h_ref[...] * inv * gm_ref[...]).astype(BF16)


def _out_proj(og, of, om, x2, wo, gm, tm=512):
    t, d = x2.shape
    row = lambda i: (i, 0)
    fixed = lambda i: (0, 0)
    return pl.pallas_call(
        _outproj_kernel,
        grid=(t // tm,),
        in_specs=[
            pl.BlockSpec((tm, og.shape[1]), row),
            pl.BlockSpec((tm, of.shape[1]), row),
            pl.BlockSpec((tm, om.shape[1]), row),
            pl.BlockSpec((tm, d), row),
            pl.BlockSpec(wo.shape, fixed),
            pl.BlockSpec((1, d), fixed),
        ],
        out_specs=[pl.BlockSpec((tm, d), row), pl.BlockSpec((tm, d), row)],
        out_shape=[jax.ShapeDtypeStruct((t, d), F32), jax.ShapeDtypeStruct((t, d), BF16)],
        compiler_params=_params(("parallel",)),
        name="out_proj",
    )(og, of, om, x2, wo, gm)


def _mlp_kernel(hn_ref, h_ref, wu_ref, wd_ref, o_ref):
    def hidden_contribution():
        tf = wu_ref.shape[1]
        piece = tf // MLP_SPLIT
        acc = None
        for s in range(MLP_SPLIT):
            cols = slice(s * piece, (s + 1) * piece)
            u = jnp.maximum(_dot(hn_ref[...], wu_ref[:, cols]), 0.0)
            part = _dot((u * u).astype(BF16), wd_ref[cols, :])
            acc = part if acc is None else acc + part
        return acc

    @pl.when(pl.program_id(1) == 0)
    def _():
        o_ref[...] = h_ref[...] + hidden_contribution()

    @pl.when(pl.program_id(1) > 0)
    def _():
        o_ref[...] += hidden_contribution()


def _mlp(hn, h, wu, wd, tm=512, tf=2048):
    t, d = h.shape
    dff = wu.shape[1]
    blocks = 2 * (2 * d * tf * 2 + tm * d * 2 + 2 * tm * d * 4)
    vmem_limit = blocks + MLP_VMEM_TEMPS
    return pl.pallas_call(
        _mlp_kernel,
        grid=(t // tm, dff // tf),
        in_specs=[
            pl.BlockSpec((tm, d), lambda i, f: (i, 0)),
            pl.BlockSpec((tm, d), lambda i, f: (i, 0)),
            pl.BlockSpec((d, tf), lambda i, f: (0, f)),
            pl.BlockSpec((tf, d), lambda i, f: (f, 0)),
        ],
        out_specs=pl.BlockSpec((tm, d), lambda i, f: (i, 0)),
        out_shape=jax.ShapeDtypeStruct((t, d), F32),
        compiler_params=_params(("parallel", "arbitrary"), vmem_limit),
        name="mlp",
    )(hn, h, wu, wd)


def _regroup_kernel(wt_ref, main_ref, small_ref):
    n_gla = OFF_FQ
    n_fox = OFF_MQ - OFF_FQ
    a0 = n_gla
    f0 = a0 + GLA_RANK
    ff0 = f0 + n_fox
    m0 = ff0 + FOX_HEADS
    n_gq = GLA_HEADS * GLA_DK
    q_scale = GLA_DK ** -0.5
    assert math.frexp(q_scale)[0] == 0.5
    main_ref[:n_gq, :] = (wt_ref[:n_gq, :] * q_scale).astype(BF16)
    main_ref[n_gq:n_gla, :] = wt_ref[n_gq:n_gla, :].astype(BF16)
    main_ref[OFF_FQ:OFF_MQ, :] = wt_ref[f0:ff0, :].astype(BF16)
    main_ref[OFF_MQ:, :] = wt_ref[m0:, :].astype(BF16)
    pad = jnp.zeros((LANES - GLA_RANK - FOX_HEADS, wt_ref.shape[1]), F32)
    small_ref[...] = jnp.concatenate(
        [wt_ref[a0:f0, :], wt_ref[ff0:m0, :], pad], axis=0).astype(BF16)


def _regroup_w_in(w_in, tc=256):
    wt = w_in.T
    n_in, d = wt.shape
    return pl.pallas_call(
        _regroup_kernel,
        grid=(d // tc,),
        in_specs=[pl.BlockSpec((n_in, tc), lambda i: (0, i))],
        out_specs=[pl.BlockSpec((N_MAIN, tc), lambda i: (0, i)),
                   pl.BlockSpec((LANES, tc), lambda i: (0, i))],
        out_shape=[jax.ShapeDtypeStruct((N_MAIN, d), BF16), jax.ShapeDtypeStruct((LANES, d), BF16)],
        compiler_params=_params(("parallel",)),
        name="regroup_w_in",
    )(wt)


def kernel(x, mem, attn_norm_g, w_in, gla_a_w2, gla_a_b, fox_f_b, fox_q_norm_g, fox_k_norm_g,
           mem_norm_g, w_mem_kv, mem_q_norm_g, mem_k_norm_g, out_norm_g, w_out, mlp_norm_g,
           w_up, w_down):
    batch, seq, d = x.shape
    n_mem = mem.shape[1]
    t = batch * seq
    x2 = x.reshape(t, d)
    row = lambda a: a.reshape(1, -1).astype(F32)

    w_main, w_small = _regroup_w_in(w_in)
    main, small, w_mem16, w_out16, w_up16, w_down16 = _in_proj(
        x2, row(attn_norm_g), w_main, w_small, (w_mem_kv, w_out, w_up, w_down))

    n_gla = GLA_HEADS * HEAD_DIM
    n_fox = FOX_HEADS * HEAD_DIM
    fb = jnp.zeros((SUBLANES, 1), F32).at[:FOX_HEADS, 0].set(fox_f_b)
    crow, aq, ak = _fox_gate(small, fb, row(fox_q_norm_g), row(fox_k_norm_g), batch, seq)
    ong_fox = out_norm_g[n_gla:n_gla + n_fox].reshape(FOX_HEADS, 1, HEAD_DIM)
    of = _fox(main, aq, ak, crow.reshape(batch, SUBLANES, 1, seq), row(fox_q_norm_g),
              row(fox_k_norm_g), ong_fox, batch, seq)

    mkn, mv = _mem_kv(mem.reshape(batch * n_mem, d), row(mem_norm_g), w_mem16,
                      row(mem_q_norm_g), row(mem_k_norm_g), batch, n_mem)
    ong_mem = out_norm_g[n_gla + n_fox:].reshape(MEM_HEADS, 1, HEAD_DIM)
    om = _mem_attn(main, mkn, mv, row(mem_q_norm_g), row(mem_k_norm_g), ong_mem, batch, seq, n_mem)

    w2p = jnp.zeros((LANES, GLA_HEADS * GLA_DK), F32).at[:GLA_RANK].set(gla_a_w2).astype(BF16)
    og = _gla(main, small, w2p, row(gla_a_b), row(out_norm_g[:n_gla]), batch, seq)

    h, hn = _out_proj(og, of, om, x2, w_out16, row(mlp_norm_g))
    y = _mlp(hn, h, w_up16, w_down16)
    return y.reshape(batch, seq, d)
```
